```python
import math
import jax, jax.numpy as jnp
from jax import lax
import numpy as np

D_MODEL = 2048
BATCH = 2
SEQ = 4096
DEPTH = 4
DEC_BATCH = 8
DEC_SEQ = 8
PAST_LEN = 16384
PAGE_SIZE = 128

N_MIXERS = 3
D_FF = 5632
PLE_DIM = 256
EPS = 1e-6
GLA_HEADS = 4
GLA_KD = D_MODEL // 2
GLA_VD = D_MODEL
GLA_DK = GLA_KD // GLA_HEADS
GLA_DV = GLA_VD // GLA_HEADS
GLA_RANK = 16
GLA_TAU = 16.0
GLA_CHUNK = 64
SB_HEADS = 16
SB_HD = D_MODEL // SB_HEADS
SB_BLOCK = 128
SB_SCALE = SB_HD ** -0.5
SB_BIAS_INIT = -6.0
POOL_WINDOWS = (2, 4, 8, 16)
POOL_GROUPS = len(POOL_WINDOWS)
POOL_GD = D_MODEL // POOL_GROUPS
POOL_STATE = max(POOL_WINDOWS) - 1
N_GLA_LAYERS = len(range(0, DEPTH, N_MIXERS))
N_SB_LAYERS = len(range(1, DEPTH, N_MIXERS))
N_POOL_LAYERS = len(range(2, DEPTH, N_MIXERS))

kernel_name = 'hybrid_gla_stickbreak_pool_macaron_step'


def rmsnorm(x, g):
    xf = x.astype(jnp.float32)
    y = xf * lax.rsqrt(jnp.mean(xf * xf, axis=-1, keepdims=True) + EPS)
    return (y * g.astype(jnp.float32)).astype(x.dtype)


def half_ffn(x, g, w_gate, w_up, w_down):
    h = rmsnorm(x, g)
    return x + 0.5 * ((jax.nn.silu(h @ w_gate) * (h @ w_up)) @ w_down)


def ple_add(x, p, g, w_gate, w_proj):
    gate = jax.nn.sigmoid(rmsnorm(x, g) @ w_gate)
    return x + gate * (p @ w_proj)


def gla_recurrence(q, k, v, log_a, s0):
    B, T, H, _ = q.shape
    DV = v.shape[-1]
    C = math.gcd(GLA_CHUNK, T)
    n = T // C
    split = lambda a: a.reshape(B, n, C, H, a.shape[-1])
    q, k, v, log_a = split(q), split(k), split(v), split(log_a)
    b = jnp.cumsum(log_a, axis=2)
    b_last = b[:, :, -1:]
    q_dec = q * jnp.exp(b)
    k_inv = k * jnp.exp(-b)
    k_end = k * jnp.exp(b_last - b)
    causal = jnp.tril(jnp.ones((C, C), dtype=bool))
    att = jnp.einsum('bnchd,bnshd->bnhcs', q_dec, k_inv)
    att = jnp.where(causal, att, 0.0)
    o_intra = jnp.einsum('bnhcs,bnshe->bnche', att, v)

    def step(S, xs):
        qd, ke, vv, dl = xs
        o = jnp.einsum('bchd,bhde->bche', qd, S)
        S = S * jnp.exp(dl)[..., None] + jnp.einsum('bchd,bche->bhde', ke, vv)
        return S, o

    xs = (jnp.moveaxis(q_dec, 1, 0), jnp.moveaxis(k_end, 1, 0),
          jnp.moveaxis(v, 1, 0), jnp.moveaxis(b_last[:, :, 0], 1, 0))
    S, o_inter = lax.scan(step, s0, xs)
    o = o_intra + jnp.moveaxis(o_inter, 0, 1)
    return o.reshape(B, T, H, DV), S


def gla_mixer(h, s0, w_in, w_a1, w_a2, b_a, g_norm, w_out):
    B, T, _ = h.shape
    q, k, v, r = jnp.split(h @ w_in, [GLA_KD, 2 * GLA_KD, 2 * GLA_KD + GLA_VD], axis=-1)
    heads = lambda a, d: a.reshape(B, T, GLA_HEADS, d).astype(jnp.float32)
    q = heads(q, GLA_DK) * GLA_DK ** -0.5
    k = heads(k, GLA_DK)
    v = heads(v, GLA_DV)
    log_a = jax.nn.log_sigmoid(((h @ w_a1) @ w_a2 + b_a).astype(jnp.float32)) / GLA_TAU
    log_a = log_a.reshape(B, T, GLA_HEADS, GLA_DK)
    o, s_new = gla_recurrence(q, k, v, log_a, s0.astype(jnp.float32))
    o = o * lax.rsqrt(jnp.mean(o * o, axis=-1, keepdims=True) + EPS) * g_norm.astype(jnp.float32)
    o = o.reshape(B, T, GLA_VD).astype(h.dtype) * jax.nn.silu(r)
    return o @ w_out, s_new.astype(s0.dtype)


def sb_project(h, w_in):
    B, T, _ = h.shape
    q, k, v = jnp.split(h @ w_in, 3, axis=-1)
    r = lambda a: a.reshape(B, T, SB_HEADS, SB_HD)
    return r(q), r(k), r(v)


def sb_weights(z, q_pos, k_pos):
    past = k_pos[None, :] < q_pos[:, None]
    log_1m = jnp.where(past, -jax.nn.softplus(z), 0.0)
    after = lax.cumsum(log_1m, axis=3, reverse=True)
    after = jnp.concatenate([after[..., 1:], jnp.zeros_like(after[..., :1])], axis=-1)
    return jnp.where(past, jnp.exp(after - jax.nn.softplus(-z)), 0.0)


def sb_prompt(h, w_in, bias, w_out):
    B, T, _ = h.shape
    q, k, v = sb_project(h, w_in)
    pos = jnp.arange(T)
    nb = T // SB_BLOCK
    qb = jnp.moveaxis(q.reshape(B, nb, SB_BLOCK, SB_HEADS, SB_HD), 1, 0)
    pb = pos.reshape(nb, SB_BLOCK)
    bh = bias.astype(jnp.float32)[None, :, None, None]

    def block(args):
        qi, pi = args
        z = jnp.einsum('bqhd,bkhd->bhqk', qi, k, preferred_element_type=jnp.float32) * SB_SCALE + bh
        w = sb_weights(z, pi, pos).astype(v.dtype)
        return jnp.einsum('bhqk,bkhd->bqhd', w, v)

    o = jnp.moveaxis(lax.map(block, (qb, pb)), 0, 1).reshape(B, T, D_MODEL)
    return o @ w_out, k, v


def sb_sample(h, cache_k, cache_v, page_table, w_in, bias, w_out):
    B, T, _ = h.shape
    q, k, v = sb_project(h, w_in)
    past = page_table.shape[1] * cache_k.shape[1]
    k_past = cache_k[page_table].reshape(B, past, SB_HEADS, SB_HD)
    v_past = cache_v[page_table].reshape(B, past, SB_HEADS, SB_HD)
    q_pos = past + jnp.arange(T)
    k_pos = jnp.arange(past + T)
    bh = bias.astype(jnp.float32)[None, :, None, None]
    z = jnp.concatenate([
        jnp.einsum('bqhd,bkhd->bhqk', q, k_past, preferred_element_type=jnp.float32),
        jnp.einsum('bqhd,bkhd->bhqk', q, k, preferred_element_type=jnp.float32)], axis=-1) * SB_SCALE + bh
    w = sb_weights(z, q_pos, k_pos).astype(v.dtype)
    o = (jnp.einsum('bhqk,bkhd->bqhd', w[..., :past], v_past)
         + jnp.einsum('bhqk,bkhd->bqhd', w[..., past:], v))
    return o.reshape(B, T, D_MODEL) @ w_out, k, v


def pool_mixer(h, prev, pos0, w_pool, scale):
    B, T, D = h.shape
    P = prev.shape[1]
    ext = jnp.concatenate([prev.astype(h.dtype), h], axis=1)
    cs = jnp.concatenate([jnp.zeros((B, 1, D), jnp.float32),
                          jnp.cumsum(ext.astype(jnp.float32), axis=1)], axis=1)
    pos = pos0 + jnp.arange(T)
    hf = h.astype(jnp.float32)
    outs = []
    for g, w in enumerate(POOL_WINDOWS):
        c = slice(g * POOL_GD, (g + 1) * POOL_GD)
        win = cs[:, P + 1:P + 1 + T, c] - cs[:, P + 1 - w:P + 1 - w + T, c]
        cnt = jnp.minimum(w, pos + 1).astype(jnp.float32)[None, :, None]
        d = (win / cnt - hf[:, :, c]).astype(h.dtype)
        outs.append(d @ w_pool[g])
    y = jnp.concatenate(outs, axis=-1) * scale
    return y, ext[:, -P:].astype(prev.dtype)


def setup_inputs(seed: int = 0) -> dict:
    key = jax.random.key(seed)
    ks = iter(jax.random.split(key, 48))
    nrm = lambda shape, s=1.0: jax.random.normal(next(ks), shape, jnp.float32) * s
    gain = lambda shape: 1.0 + 0.02 * nrm(shape)
    n_pages = PAST_LEN // PAGE_SIZE
    n_phys = (5 * DEC_BATCH * n_pages) // 4
    page_table = jax.random.permutation(next(ks), n_phys)[:DEC_BATCH * n_pages]
    page_table = page_table.reshape(DEC_BATCH, n_pages).astype(jnp.int32)
    D, F = D_MODEL, D_FF
    return {
        'x_prompt': nrm((BATCH, SEQ, D)),
        'x_sample': nrm((DEC_BATCH, DEC_SEQ, D)),
        'state_gla': nrm((N_GLA_LAYERS, DEC_BATCH, GLA_HEADS, GLA_DK, GLA_DV)),
        'cache_sb_k': nrm((N_SB_LAYERS, n_phys, PAGE_SIZE, SB_HEADS, SB_HD)),
        'cache_sb_v': nrm((N_SB_LAYERS, n_phys, PAGE_SIZE, SB_HEADS, SB_HD)),
        'state_pool': nrm((N_POOL_LAYERS, DEC_BATCH, POOL_STATE, D)),
        'page_table': page_table,
        'p_prompt': nrm((DEPTH, BATCH, SEQ, PLE_DIM)),
        'p_sample': nrm((DEPTH, DEC_BATCH, DEC_SEQ, PLE_DIM)),
        'norm_ffn1': gain((DEPTH, D)),
        'ffn1_gate': nrm((DEPTH, D, F), D ** -0.5),
        'ffn1_up': nrm((DEPTH, D, F), D ** -0.5),
        'ffn1_down': nrm((DEPTH, F, D), F ** -0.5),
        'norm_mix': gain((DEPTH, D)),
        'norm_ffn2': gain((DEPTH, D)),
        'ffn2_gate': nrm((DEPTH, D, F), D ** -0.5),
        'ffn2_up': nrm((DEPTH, D, F), D ** -0.5),
        'ffn2_down': nrm((DEPTH, F, D), F ** -0.5),
        'norm_ple': gain((DEPTH, D)),
        'ple_gate': nrm((DEPTH, D, D), D ** -0.5),
        'ple_proj': nrm((DEPTH, PLE_DIM, D), PLE_DIM ** -0.5),
        'gla_in': nrm((N_GLA_LAYERS, D, 2 * GLA_KD + 2 * GLA_VD), D ** -0.5),
        'gla_a1': nrm((N_GLA_LAYERS, D, GLA_RANK), D ** -0.5),
        'gla_a2': nrm((N_GLA_LAYERS, GLA_RANK, GLA_KD), GLA_RANK ** -0.5),
        'gla_a_bias': nrm((N_GLA_LAYERS, GLA_KD), 0.1),
        'gla_norm': gain((N_GLA_LAYERS, GLA_HEADS, GLA_DV)),
        'gla_out': nrm((N_GLA_LAYERS, GLA_VD, D), GLA_VD ** -0.5),
        'sb_in': nrm((N_SB_LAYERS, D, 3 * D), D ** -0.5),
        'sb_bias': SB_BIAS_INIT + 0.1 * nrm((N_SB_LAYERS, SB_HEADS)),
        'sb_out': nrm((N_SB_LAYERS, D, D), D ** -0.5),
        'pool_w': nrm((N_POOL_LAYERS, POOL_GROUPS, POOL_GD, POOL_GD), POOL_GD ** -0.5),
        'pool_scale': gain((N_POOL_LAYERS, D)),
        'norm_final': gain((D,)),
    }


def reference(x_prompt, x_sample, state_gla, cache_sb_k, cache_sb_v, state_pool, page_table,
              p_prompt, p_sample, norm_ffn1, ffn1_gate, ffn1_up, ffn1_down, norm_mix,
              norm_ffn2, ffn2_gate, ffn2_up, ffn2_down, norm_ple, ple_gate, ple_proj,
              gla_in, gla_a1, gla_a2, gla_a_bias, gla_norm, gla_out, sb_in, sb_bias, sb_out,
              pool_w, pool_scale, norm_final):
    xp, xs = x_prompt, x_sample
    bp = x_prompt.shape[0]
    gla_p, gla_s, kp, vp, ksm, vsm, pool_p, pool_s = [], [], [], [], [], [], [], []
    for i in range(DEPTH):
        kind, j = i % N_MIXERS, i // N_MIXERS
        xp = half_ffn(xp, norm_ffn1[i], ffn1_gate[i], ffn1_up[i], ffn1_down[i])
        xs = half_ffn(xs, norm_ffn1[i], ffn1_gate[i], ffn1_up[i], ffn1_down[i])
        hp = rmsnorm(xp, norm_mix[i])
        hs = rmsnorm(xs, norm_mix[i])
        if kind == 0:
            s0 = jnp.zeros((bp, GLA_HEADS, GLA_DK, GLA_DV), state_gla.dtype)
            mp, st_p = gla_mixer(hp, s0, gla_in[j], gla_a1[j], gla_a2[j], gla_a_bias[j], gla_norm[j], gla_out[j])
            ms, st_s = gla_mixer(hs, state_gla[j], gla_in[j], gla_a1[j], gla_a2[j], gla_a_bias[j], gla_norm[j], gla_out[j])
            gla_p.append(st_p)
            gla_s.append(st_s)
        elif kind == 1:
            mp, k_p, v_p = sb_prompt(hp, sb_in[j], sb_bias[j], sb_out[j])
            ms, k_s, v_s = sb_sample(hs, cache_sb_k[j], cache_sb_v[j], page_table, sb_in[j], sb_bias[j], sb_out[j])
            kp.append(k_p)
            vp.append(v_p)
            ksm.append(k_s)
            vsm.append(v_s)
        else:
            prev0 = jnp.zeros((bp, POOL_STATE, D_MODEL), state_pool.dtype)
            mp, st_p = pool_mixer(hp, prev0, 0, pool_w[j], pool_scale[j])
            ms, st_s = pool_mixer(hs, state_pool[j], PAST_LEN, pool_w[j], pool_scale[j])
            pool_p.append(st_p)
            pool_s.append(st_s)
        xp = xp + mp
        xs = xs + ms
        xp = half_ffn(xp, norm_ffn2[i], ffn2_gate[i], ffn2_up[i], ffn2_down[i])
        xs = half_ffn(xs, norm_ffn2[i], ffn2_gate[i], ffn2_up[i], ffn2_down[i])
        xp = ple_add(xp, p_prompt[i], norm_ple[i], ple_gate[i], ple_proj[i])
        xs = ple_add(xs, p_sample[i], norm_ple[i], ple_gate[i], ple_proj[i])
    y_prompt = rmsnorm(xp, norm_final)
    y_sample = rmsnorm(xs, norm_final)
    return (y_prompt, y_sample, jnp.stack(gla_p), jnp.stack(gla_s), jnp.stack(kp), jnp.stack(vp),
            jnp.stack(ksm), jnp.stack(vsm), jnp.stack(pool_p), jnp.stack(pool_s))
```

```python
import functools

import jax
import jax.numpy as jnp
from jax import lax
from jax.experimental import pallas as pl
from jax.experimental.pallas import tpu as pltpu

f32 = jnp.float32
bf16 = jnp.bfloat16

EPS = 1e-6
N_MIXERS = 3
GLA_HEADS = 4
GLA_TAU = 16.0
GLA_CHUNK = 64
GLA_RANK_PAD = 128
SB_HEADS = 16
POOL_WINDOWS = (2, 4, 8, 16)
POOL_HALO = 16
VMEM_LIMIT = 56 * 1024 * 1024


def _cparams(sem):
    return pltpu.CompilerParams(dimension_semantics=sem, vmem_limit_bytes=VMEM_LIMIT)


def _rms(x, g):
    return x * lax.rsqrt(jnp.mean(x * x, axis=-1, keepdims=True) + EPS) * g


def _softplus_pair(z):
    t = jnp.log1p(jnp.exp(-jnp.abs(z)))
    return jnp.maximum(z, 0.0) + t, t - jnp.minimum(z, 0.0)


def _ffn_kernel(x_ref, g_ref, wg_ref, wu_ref, wd_ref, o_ref, h_ref):
    @pl.when(pl.program_id(1) == 0)
    def _():
        x = x_ref[...]
        h_ref[...] = _rms(x, g_ref[...]).astype(bf16)
        o_ref[...] = x

    h = h_ref[...]
    a = jnp.dot(h, wg_ref[...], preferred_element_type=f32)
    u = jnp.dot(h, wu_ref[...], preferred_element_type=f32)
    act = (0.5 * a * jax.nn.sigmoid(a) * u).astype(bf16)
    o_ref[...] += jnp.dot(act, wd_ref[...], preferred_element_type=f32)


def _ffn(x, g, wg, wu, wd, layer, bm, bf=512):
    m, d = x.shape
    f = wg.shape[-1]
    return pl.pallas_call(
        _ffn_kernel,
        grid=(m // bm, f // bf),
        in_specs=[
            pl.BlockSpec((bm, d), lambda i, j: (i, 0)),
            pl.BlockSpec((None, 1, d), lambda i, j: (layer, 0, 0)),
            pl.BlockSpec((None, d, bf), lambda i, j: (layer, 0, j)),
            pl.BlockSpec((None, d, bf), lambda i, j: (layer, 0, j)),
            pl.BlockSpec((None, bf, d), lambda i, j: (layer, j, 0)),
        ],
        out_specs=pl.BlockSpec((bm, d), lambda i, j: (i, 0)),
        out_shape=jax.ShapeDtypeStruct((m, d), f32),
        scratch_shapes=[pltpu.VMEM((bm, d), bf16)],
        compiler_params=_cparams(("parallel", "arbitrary")),
        name="ffn",
    )(x, g, wg, wu, wd)


def _ple_kernel(x_ref, p_ref, g_ref, wg_ref, wp_ref, gf_ref, o_ref, *, final):
    x = x_ref[...]
    h = _rms(x, g_ref[...]).astype(bf16)
    gate = jax.nn.sigmoid(jnp.dot(h, wg_ref[...], preferred_element_type=f32))
    proj = jnp.dot(p_ref[...].astype(bf16), wp_ref[...], preferred_element_type=f32)
    y = x + gate * proj
    if final:
        y = _rms(y, gf_ref[...])
    o_ref[...] = y


def _ple(x, p, g, wg, wp, gf, layer, bm, final):
    m, d = x.shape
    pd = p.shape[-1]
    return pl.pallas_call(
        functools.partial(_ple_kernel, final=final),
        grid=(m // bm,),
        in_specs=[
            pl.BlockSpec((bm, d), lambda i: (i, 0)),
            pl.BlockSpec((None, bm, pd), lambda i: (layer, i, 0)),
            pl.BlockSpec((None, 1, d), lambda i: (layer, 0, 0)),
            pl.BlockSpec((None, d, d), lambda i: (layer, 0, 0)),
            pl.BlockSpec((None, pd, d), lambda i: (layer, 0, 0)),
            pl.BlockSpec((1, d), lambda i: (0, 0)),
        ],
        out_specs=pl.BlockSpec((bm, d), lambda i: (i, 0)),
        out_shape=jax.ShapeDtypeStruct((m, d), f32),
        compiler_params=_cparams(("parallel",)),
        name="ple",
    )(x, p, g, wg, wp, gf)


def _norm_mm_kernel(x_ref, g_ref, w_ref, o_ref, h_ref):
    @pl.when(pl.program_id(1) == 0)
    def _():
        h_ref[...] = _rms(x_ref[...], g_ref[...]).astype(bf16)

    o_ref[...] = jnp.dot(h_ref[...], w_ref[...], preferred_element_type=f32)


def _norm_mm(x, g, w, layer, wlayer, bm, bn):
    m, d = x.shape
    n = w.shape[-1]
    return pl.pallas_call(
        _norm_mm_kernel,
        grid=(m // bm, n // bn),
        in_specs=[
            pl.BlockSpec((bm, d), lambda i, j: (i, 0)),
            pl.BlockSpec((None, 1, d), lambda i, j: (layer, 0, 0)),
            pl.BlockSpec((None, d, bn), lambda i, j: (wlayer, 0, j)),
        ],
        out_specs=pl.BlockSpec((bm, bn), lambda i, j: (i, j)),
        out_shape=jax.ShapeDtypeStruct((m, n), f32),
        scratch_shapes=[pltpu.VMEM((bm, d), bf16)],
        compiler_params=_cparams(("parallel", "arbitrary")),
        name="norm_mm",
    )(x, g, w)


def _mm_res_kernel(a_ref, w_ref, x_ref, o_ref):
    o_ref[...] = x_ref[...] + jnp.dot(a_ref[...], w_ref[...], preferred_element_type=f32)


def _mm_res(a, w, x, wlayer, bm, bn):
    m, k = a.shape
    n = w.shape[-1]
    return pl.pallas_call(
        _mm_res_kernel,
        grid=(m // bm, n // bn),
        in_specs=[
            pl.BlockSpec((bm, k), lambda i, j: (i, 0)),
            pl.BlockSpec((None, k, bn), lambda i, j: (wlayer, 0, j)),
            pl.BlockSpec((bm, bn), lambda i, j: (i, j)),
        ],
        out_specs=pl.BlockSpec((bm, bn), lambda i, j: (i, j)),
        out_shape=jax.ShapeDtypeStruct((m, n), f32),
        compiler_params=_cparams(("parallel", "arbitrary")),
        name="mm_res",
    )(a, w, x)


def _split3(x):
    hi = x.astype(bf16)
    r1 = x - hi.astype(f32)
    mid = r1.astype(bf16)
    lo = (r1 - mid.astype(f32)).astype(bf16)
    return hi, mid, lo


def _gla_kernel(q_ref, k_ref, v_ref, r_ref, a1_ref, wa2_ref, ba_ref, gn_ref, s0_ref,
                og_ref, sout_ref, s_ref, *, chunk, n_chunks, valid, qscale):
    c_rows = chunk

    @pl.when(pl.program_id(2) == 0)
    def _():
        s_ref[...] = s0_ref[0, 0]

    row = lax.broadcasted_iota(jnp.int32, (c_rows, c_rows), 0)
    col = lax.broadcasted_iota(jnp.int32, (c_rows, c_rows), 1)
    tril = col <= row
    tri_b = tril.astype(bf16)
    ones_b = jnp.ones((c_rows, 128), bf16)
    wa2 = wa2_ref[...]
    ba = ba_ref[...]
    gn = gn_ref[...]
    tn = (((0,), (0,)), ((), ()))
    nt = (((1,), (1,)), ((), ()))

    def body(c, carry):
        sl = pl.ds(pl.multiple_of(c * c_rows, c_rows), c_rows)
        x = jnp.dot(a1_ref[0, sl, :].astype(bf16), wa2, preferred_element_type=f32) + ba
        la = (jnp.minimum(x, 0.0) - jnp.log1p(jnp.exp(-jnp.abs(x)))) * (1.0 / GLA_TAU)
        if valid < c_rows:
            la = jnp.where(lax.broadcasted_iota(jnp.int32, la.shape, 0) < valid, la, 0.0)
        parts = _split3(la)
        b = sum(jnp.dot(tri_b, p, preferred_element_type=f32) for p in parts)
        dl_col = sum(lax.dot_general(p, ones_b, tn, preferred_element_type=f32) for p in parts)
        b_last = b[c_rows - 1:c_rows, :]
        q = q_ref[0, sl, :] * qscale
        k = k_ref[0, sl, :]
        vb = v_ref[0, sl, :].astype(bf16)
        qd = (q * jnp.exp(b)).astype(bf16)
        ki = (k * jnp.exp(-b)).astype(bf16)
        ke = (k * jnp.exp(b_last - b)).astype(bf16)
        att = lax.dot_general(qd, ki, nt, preferred_element_type=f32)
        att = jnp.where(tril, att, 0.0).astype(bf16)
        s_old = s_ref[...]
        o = (jnp.dot(att, vb, preferred_element_type=f32)
             + jnp.dot(qd, s_old.astype(bf16), preferred_element_type=f32))
        decay = jnp.exp(dl_col)
        dv = s_old.shape[1]
        decay = jnp.concatenate([decay] * (dv // 128), axis=1)
        s_ref[...] = s_old * decay + lax.dot_general(ke, vb, tn, preferred_element_type=f32)
        o = o * lax.rsqrt(jnp.mean(o * o, axis=-1, keepdims=True) + EPS) * gn
        r = r_ref[0, sl, :]
        og_ref[0, sl, :] = (o * (r * jax.nn.sigmoid(r))).astype(og_ref.dtype)
        return carry

    lax.fori_loop(0, n_chunks, body, 0)

    @pl.when(pl.program_id(2) == pl.num_programs(2) - 1)
    def _():
        sout_ref[0, 0] = s_ref[...]


def _gla(proj, wa2, ba, gn, s0, layer, tb, valid):
    bsz, t, _ = proj.shape
    heads = GLA_HEADS
    dk, dv = s0.shape[-2], s0.shape[-1]
    kd, vd = heads * dk, heads * dv
    a1_blk = (2 * kd + 2 * vd) // GLA_RANK_PAD
    kern = functools.partial(_gla_kernel, chunk=GLA_CHUNK, n_chunks=tb // GLA_CHUNK, valid=valid,
                             qscale=float(dk) ** -0.5)
    return pl.pallas_call(
        kern,
        grid=(bsz, heads, t // tb),
        in_specs=[
            pl.BlockSpec((1, tb, dk), lambda b, h, i: (b, i, h)),
            pl.BlockSpec((1, tb, dk), lambda b, h, i: (b, i, heads + h)),
            pl.BlockSpec((1, tb, dv), lambda b, h, i: (b, i, 2 * kd // dv + h)),
            pl.BlockSpec((1, tb, dv), lambda b, h, i: (b, i, (2 * kd + vd) // dv + h)),
            pl.BlockSpec((1, tb, GLA_RANK_PAD), lambda b, h, i: (b, i, a1_blk)),
            pl.BlockSpec((None, GLA_RANK_PAD, dk), lambda b, h, i: (layer, 0, h)),
            pl.BlockSpec((None, 1, dk), lambda b, h, i: (layer, 0, h)),
            pl.BlockSpec((None, 1, dv), lambda b, h, i: (layer, 0, h)),
            pl.BlockSpec((1, 1, dk, dv), lambda b, h, i: (b, h, 0, 0)),
        ],
        out_specs=[
            pl.BlockSpec((1, tb, dv), lambda b, h, i: (b, i, h)),
            pl.BlockSpec((1, 1, dk, dv), lambda b, h, i: (b, h, 0, 0)),
        ],
        out_shape=[
            jax.ShapeDtypeStruct((bsz, t, vd), bf16),
            jax.ShapeDtypeStruct((bsz, heads, dk, dv), f32),
        ],
        scratch_shapes=[pltpu.VMEM((dk, dv), f32)],
        compiler_params=_cparams(("parallel", "parallel", "arbitrary")),
        name="gla",
    )(proj, proj, proj, proj, proj, wa2, ba, gn, s0)


def _sbp_kernel(q_ref, k_ref, v_ref, bias_ref, o_ref, *, bq, bk, scale):
    i = pl.program_id(2)
    q = q_ref[0].astype(bf16)
    hd = q.shape[-1]
    bias = bias_ref[0]
    rj = lax.broadcasted_iota(jnp.int32, (bk, 2 * bk), 0)
    cs = lax.broadcasted_iota(jnp.int32, (bk, 2 * bk), 1)
    tri_ext = jnp.where((cs >= bk) | (rj > cs), 1.0, 0.0).astype(bf16)
    col_minus_row = (lax.broadcasted_iota(jnp.int32, (bq, bk), 1)
                     - lax.broadcasted_iota(jnp.int32, (bq, bk), 0))
    nt = (((1,), (1,)), ((), ()))
    ratio = bq // bk

    def step(kb, carry, acc, masked):
        ks = pl.ds(pl.multiple_of(kb * bk, bk), bk)
        k = k_ref[0, ks, :].astype(bf16)
        v = v_ref[0, ks, :].astype(bf16)
        z = lax.dot_general(q, k, nt, preferred_element_type=f32) * scale + bias
        sp, spn = _softplus_pair(z)
        l1m = -sp
        if masked:
            past = col_minus_row < (i * bq - kb * bk)
            l1m = jnp.where(past, l1m, 0.0)
        res = jnp.dot(l1m.astype(bf16), tri_ext, preferred_element_type=f32)
        a = jnp.exp(carry + res[:, :bk] - spn)
        if masked:
            a = jnp.where(past, a, 0.0)
        acc = acc + jnp.dot(a.astype(bf16), v, preferred_element_type=f32)
        return carry + res[:, bk:], acc

    carry = jnp.zeros((bq, bk), f32)
    acc = jnp.zeros((bq, hd), f32)
    for d in range(ratio - 1, -1, -1):
        carry, acc = step(i * ratio + d, carry, acc, True)

    def body(n, ca):
        return step(i * ratio - 1 - n, ca[0], ca[1], False)

    carry, acc = lax.fori_loop(0, i * ratio, body, (carry, acc))
    o_ref[0] = acc.astype(o_ref.dtype)


def _sb_prompt(qkv, bias, heads, bq=256, bk=128):
    bsz, t, d3 = qkv.shape
    d = d3 // 3
    hd = d // heads
    kern = functools.partial(_sbp_kernel, bq=bq, bk=bk, scale=float(hd) ** -0.5)
    return pl.pallas_call(
        kern,
        grid=(bsz, heads, t // bq),
        in_specs=[
            pl.BlockSpec((1, bq, hd), lambda b, h, i: (b, i, h)),
            pl.BlockSpec((1, t, hd), lambda b, h, i: (b, 0, heads + h)),
            pl.BlockSpec((1, t, hd), lambda b, h, i: (b, 0, 2 * heads + h)),
            pl.BlockSpec((1, 1, bk), lambda b, h, i: (h, 0, 0)),
        ],
        out_specs=pl.BlockSpec((1, bq, hd), lambda b, h, i: (b, i, h)),
        out_shape=jax.ShapeDtypeStruct((bsz, t, d), bf16),
        compiler_params=_cparams(("parallel", "parallel", "arbitrary")),
        name="sb_prompt",
    )(qkv, qkv, qkv, bias)


def _sbs_kernel(pt_ref, q_ref, kn_ref, vn_ref, kc_ref, vc_ref, bias_ref, o_ref,
                qm_ref, acc_ref, carry_ref, *, heads, nq, scale):
    p = pl.program_id(1)
    ps, d = kc_ref.shape[1], kc_ref.shape[2]
    hd = d // heads
    nc = heads * nq
    bias = bias_ref[...]
    rs = lax.broadcasted_iota(jnp.int32, (ps, ps), 0)
    cj = lax.broadcasted_iota(jnp.int32, (ps, ps), 1)
    tri = (cj > rs).astype(bf16)
    nt = (((1,), (1,)), ((), ()))
    tn = (((0,), (0,)), ((), ()))

    def page(kp, vp, masked):
        z = lax.dot_general(kp.astype(bf16), qm_ref[...], nt, preferred_element_type=f32) * scale + bias
        sp, spn = _softplus_pair(z)
        l1m = -sp
        if masked:
            key = lax.broadcasted_iota(jnp.int32, (ps, nc), 0)
            qry = lax.broadcasted_iota(jnp.int32, (ps, nc), 1) % nq
            past = key < qry
            l1m = jnp.where(past, l1m, 0.0)
        after = jnp.dot(tri, l1m.astype(bf16), preferred_element_type=f32) + carry_ref[0:1, :]
        w = jnp.exp(after - spn)
        if masked:
            w = jnp.where(past, w, 0.0)
        acc_ref[...] += lax.dot_general(w.astype(bf16), vp.astype(bf16), tn, preferred_element_type=f32)
        carry_ref[...] = carry_ref[...] + jnp.sum(l1m, axis=0, keepdims=True)

    @pl.when(p == 0)
    def _():
        q = q_ref[0]
        qrep = jnp.concatenate([q] * heads, axis=0)
        rh = lax.broadcasted_iota(jnp.int32, (nc, d), 0) // nq
        ch = lax.broadcasted_iota(jnp.int32, (nc, d), 1) // hd
        qm_ref[...] = jnp.where(rh == ch, qrep, 0.0).astype(bf16)
        acc_ref[...] = jnp.zeros_like(acc_ref)
        carry_ref[...] = jnp.zeros_like(carry_ref)
        page(kn_ref[0], vn_ref[0], True)

    page(kc_ref[0], vc_ref[0], False)

    @pl.when(p == pl.num_programs(1) - 1)
    def _():
        for h in range(heads):
            o_ref[0, :, h * hd:(h + 1) * hd] = acc_ref[h * nq:(h + 1) * nq, h * hd:(h + 1) * hd].astype(o_ref.dtype)


def _sb_sample(q, k_new, v_new, cache_k, cache_v, page_table, bias, layer, heads):
    bsz, nq, d = q.shape
    n_pages = page_table.shape[1]
    ps = cache_k.shape[2]
    nc = heads * nq
    kern = functools.partial(_sbs_kernel, heads=heads, nq=nq, scale=float(d // heads) ** -0.5)
    grid_spec = pltpu.PrefetchScalarGridSpec(
        num_scalar_prefetch=1,
        grid=(bsz, n_pages),
        in_specs=[
            pl.BlockSpec((1, nq, d), lambda b, p, pt: (b, 0, 0)),
            pl.BlockSpec((1, ps, d), lambda b, p, pt: (b, 0, 0)),
            pl.BlockSpec((1, ps, d), lambda b, p, pt: (b, 0, 0)),
            pl.BlockSpec((None, 1, ps, d), lambda b, p, pt: (layer, pt[b, n_pages - 1 - p], 0, 0)),
            pl.BlockSpec((None, 1, ps, d), lambda b, p, pt: (layer, pt[b, n_pages - 1 - p], 0, 0)),
            pl.BlockSpec((1, nc), lambda b, p, pt: (0, 0)),
        ],
        out_specs=pl.BlockSpec((1, nq, d), lambda b, p, pt: (b, 0, 0)),
        scratch_shapes=[
            pltpu.VMEM((nc, d), bf16),
            pltpu.VMEM((nc, d), f32),
            pltpu.VMEM((8, nc), f32),
        ],
    )
    return pl.pallas_call(
        kern,
        grid_spec=grid_spec,
        out_shape=jax.ShapeDtypeStruct((bsz, nq, d), bf16),
        compiler_params=_cparams(("parallel", "arbitrary")),
        name="sb_sample",
    )(page_table, q, k_new, v_new, cache_k, cache_v, bias)


def _pool_kernel(*refs, bm, has_halo, pos0, windows):
    if has_halo:
        x_ref, xh_ref, st_ref, g_ref, w_ref, sc_ref, o_ref, so_ref, ext_ref = refs
    else:
        x_ref, st_ref, g_ref, w_ref, sc_ref, o_ref, so_ref, ext_ref = refs
    i = pl.program_id(1)
    g = g_ref[...]
    x = x_ref[0]
    h = _rms(x, g)
    if has_halo:
        halo = jnp.where(i == 0, st_ref[0], _rms(xh_ref[0], g))
    else:
        halo = st_ref[0]
    ext_ref[0:POOL_HALO, :] = halo
    ext_ref[POOL_HALO:POOL_HALO + bm, :] = h
    gd = x.shape[-1] // len(windows)
    pos = pos0 + i * bm + lax.broadcasted_iota(jnp.int32, (bm, 1), 0)
    for gi, w in enumerate(windows):
        cs = slice(gi * gd, (gi + 1) * gd)
        hg = h[:, cs]
        win = hg
        for s in range(1, w):
            win = win + ext_ref[pl.ds(POOL_HALO - s, bm), cs]
        cnt = jnp.minimum(w, pos + 1).astype(f32)
        dlt = (win / cnt - hg).astype(bf16)
        y = jnp.dot(dlt, w_ref[gi], preferred_element_type=f32)
        o_ref[0, :, cs] = x[:, cs] + y * sc_ref[:, cs]
    so_ref[0] = ext_ref[pl.ds(bm, POOL_HALO), :]


def _pool(x, state, g, w, sc, layer, wlayer, bm, pos0):
    bsz, t, d = x.shape
    ng, gd = w.shape[1], w.shape[2]
    has_halo = t > bm
    hb = bm // POOL_HALO
    kern = functools.partial(_pool_kernel, bm=bm, has_halo=has_halo, pos0=pos0, windows=POOL_WINDOWS)
    in_specs = [pl.BlockSpec((1, bm, d), lambda b, i: (b, i, 0))]
    args = [x]
    if has_halo:
        in_specs.append(pl.BlockSpec((1, POOL_HALO, d), lambda b, i: (b, jnp.maximum(i * hb - 1, 0), 0)))
        args.append(x)
    in_specs += [
        pl.BlockSpec((1, POOL_HALO, d), lambda b, i: (b, 0, 0)),
        pl.BlockSpec((None, 1, d), lambda b, i: (layer, 0, 0)),
        pl.BlockSpec((None, ng, gd, gd), lambda b, i: (wlayer, 0, 0, 0)),
        pl.BlockSpec((None, 1, d), lambda b, i: (wlayer, 0, 0)),
    ]
    args += [state, g, w, sc]
    return pl.pallas_call(
        kern,
        grid=(bsz, t // bm),
        in_specs=in_specs,
        out_specs=[
            pl.BlockSpec((1, bm, d), lambda b, i: (b, i, 0)),
            pl.BlockSpec((1, POOL_HALO, d), lambda b, i: (b, 0, 0)),
        ],
        out_shape=[
            jax.ShapeDtypeStruct((bsz, t, d), f32),
            jax.ShapeDtypeStruct((bsz, POOL_HALO, d), f32),
        ],
        scratch_shapes=[pltpu.VMEM((POOL_HALO + bm, d), f32)],
        compiler_params=_cparams(("parallel", "arbitrary")),
        name="pool",
    )(*args)


def kernel(x_prompt, x_sample, state_gla, cache_sb_k, cache_sb_v, state_pool, page_table, p_prompt, p_sample, norm_ffn1, ffn1_gate, ffn1_up, ffn1_down, norm_mix, norm_ffn2, ffn2_gate, ffn2_up, ffn2_down, norm_ple, ple_gate, ple_proj, gla_in, gla_a1, gla_a2, gla_a_bias, gla_norm, gla_out, sb_in, sb_bias, sb_out, pool_w, pool_scale, norm_final):
    bp, t, d = x_prompt.shape
    bs, ts, _ = x_sample.shape
    depth = norm_ffn1.shape[0]
    mp, ms = bp * t, bs * ts
    kd = gla_a2.shape[-1]
    dk = kd // GLA_HEADS
    dv = gla_out.shape[1] // GLA_HEADS
    hd = d // SB_HEADS
    page = cache_sb_k.shape[2]
    pstate = state_pool.shape[2]

    row3 = lambda a: a.reshape(a.shape[0], 1, a.shape[-1])
    n_ffn1, n_mix, n_ffn2, n_ple = row3(norm_ffn1), row3(norm_mix), row3(norm_ffn2), row3(norm_ple)
    n_final = norm_final.reshape(1, d)
    w1g, w1u, w1d = ffn1_gate.astype(bf16), ffn1_up.astype(bf16), ffn1_down.astype(bf16)
    w2g, w2u, w2d = ffn2_gate.astype(bf16), ffn2_up.astype(bf16), ffn2_down.astype(bf16)
    wpg, wpp = ple_gate.astype(bf16), ple_proj.astype(bf16)
    n_gla = gla_in.shape[0]
    gla_w = jnp.concatenate(
        [gla_in, gla_a1, jnp.zeros((n_gla, d, GLA_RANK_PAD - gla_a1.shape[-1]), f32)], axis=-1).astype(bf16)
    gla_wa2 = jnp.concatenate(
        [gla_a2, jnp.zeros((n_gla, GLA_RANK_PAD - gla_a2.shape[1], kd), f32)], axis=1).astype(bf16)
    gla_ba = row3(gla_a_bias)
    gla_gn = gla_norm.reshape(n_gla, 1, GLA_HEADS * dv)
    gla_wo = gla_out.astype(bf16)
    sb_wi, sb_wo = sb_in.astype(bf16), sb_out.astype(bf16)
    pool_wb = pool_w.astype(bf16)
    pool_sc = row3(pool_scale)
    pp = p_prompt.reshape(depth, mp, -1)
    psm = p_sample.reshape(depth, ms, -1)

    xp = x_prompt.reshape(mp, d)
    xs = x_sample.reshape(ms, d)
    gla_p, gla_s, kp, vp, ksm, vsm, pool_p, pool_s = [], [], [], [], [], [], [], []
    for i in range(depth):
        kind, j = i % N_MIXERS, i // N_MIXERS
        xp = _ffn(xp, n_ffn1, w1g, w1u, w1d, i, bm=512)
        xs = _ffn(xs, n_ffn1, w1g, w1u, w1d, i, bm=ms)
        if kind == 0:
            proj_p = _norm_mm(xp, n_mix, gla_w, i, j, bm=512, bn=896).reshape(bp, t, -1)
            proj_s = _norm_mm(xs, n_mix, gla_w, i, j, bm=ms, bn=896).reshape(bs, ts, -1)
            proj_s = jnp.pad(proj_s, ((0, 0), (0, GLA_CHUNK - ts), (0, 0)))
            s0 = jnp.zeros((bp, GLA_HEADS, dk, dv), f32)
            og_p, st_p = _gla(proj_p, gla_wa2, gla_ba, gla_gn, s0, j, tb=512, valid=GLA_CHUNK)
            og_s, st_s = _gla(proj_s, gla_wa2, gla_ba, gla_gn, state_gla[j], j, tb=GLA_CHUNK, valid=ts)
            xp = _mm_res(og_p.reshape(mp, -1), gla_wo, xp, j, bm=512, bn=512)
            xs = _mm_res(og_s[:, :ts].reshape(ms, -1), gla_wo, xs, j, bm=ms, bn=512)
            gla_p.append(st_p)
            gla_s.append(st_s)
        elif kind == 1:
            qkv_p = _norm_mm(xp, n_mix, sb_wi, i, j, bm=512, bn=512)
            qkv_s = _norm_mm(xs, n_mix, sb_wi, i, j, bm=ms, bn=512)
            bias_p = jnp.broadcast_to(sb_bias[j][:, None, None], (SB_HEADS, 1, 128))
            o_p = _sb_prompt(qkv_p.reshape(bp, t, 3 * d), bias_p, SB_HEADS)
            k_s = qkv_s[:, d:2 * d].reshape(bs, ts, d)
            v_s = qkv_s[:, 2 * d:].reshape(bs, ts, d)
            pad = ((0, 0), (0, page - ts), (0, 0))
            bias_s = jnp.repeat(sb_bias[j], ts)[None, :]
            o_s = _sb_sample(qkv_s[:, :d].reshape(bs, ts, d), jnp.pad(k_s, pad), jnp.pad(v_s, pad),
                             cache_sb_k.reshape(cache_sb_k.shape[:3] + (d,)),
                             cache_sb_v.reshape(cache_sb_v.shape[:3] + (d,)),
                             page_table, bias_s, j, SB_HEADS)
            xp = _mm_res(o_p.reshape(mp, d), sb_wo, xp, j, bm=512, bn=512)
            xs = _mm_res(o_s.reshape(ms, d), sb_wo, xs, j, bm=ms, bn=512)
            kp.append(qkv_p[:, d:2 * d].reshape(bp, t, SB_HEADS, hd))
            vp.append(qkv_p[:, 2 * d:].reshape(bp, t, SB_HEADS, hd))
            ksm.append(k_s.reshape(bs, ts, SB_HEADS, hd))
            vsm.append(v_s.reshape(bs, ts, SB_HEADS, hd))
        else:
            st0_p = jnp.zeros((bp, POOL_HALO, d), f32)
            st0_s = jnp.pad(state_pool[j], ((0, 0), (POOL_HALO - pstate, 0), (0, 0)))
            xp3, so_p = _pool(xp.reshape(bp, t, d), st0_p, n_mix, pool_wb, pool_sc, i, j, bm=512, pos0=0)
            xs3, so_s = _pool(xs.reshape(bs, ts, d), st0_s, n_mix, pool_wb, pool_sc, i, j, bm=ts,
                              pos0=page_table.shape[1] * page)
            xp, xs = xp3.reshape(mp, d), xs3.reshape(ms, d)
            pool_p.append(so_p[:, POOL_HALO - pstate:])
            pool_s.append(so_s[:, POOL_HALO - pstate:])
        xp = _ffn(xp, n_ffn2, w2g, w2u, w2d, i, bm=512)
        xs = _ffn(xs, n_ffn2, w2g, w2u, w2d, i, bm=ms)
        final = i == depth - 1
        xp = _ple(xp, pp, n_ple, wpg, wpp, n_final, i, bm=256, final=final)
        xs = _ple(xs, psm, n_ple, wpg, wpp, n_final, i, bm=ms, final=final)
    return (xp.reshape(bp, t, d), xs.reshape(bs, ts, d), jnp.stack(gla_p), jnp.stack(gla_s),
            jnp.stack(kp), jnp.stack(vp), jnp.stack(ksm), jnp.stack(vsm),
            jnp.stack(pool_p), jnp.stack(pool_s))
```

```python
import functools

import jax
import jax.numpy as jnp
from jax import lax
from jax.experimental import pallas as pl
from jax.experimental.pallas import tpu as pltpu

f32 = jnp.float32
bf16 = jnp.bfloat16

EPS = 1e-6
N_MIXERS = 3
GLA_HEADS = 4
GLA_TAU = 16.0
GLA_CHUNK = 64
GLA_RANK_PAD = 128
SB_HEADS = 16
SB_KEYS = 256
SB_SUB = 128
POOL_WINDOWS = (2, 4, 8, 16)
POOL_HALO = 16
VMEM_LIMIT = 56 * 1024 * 1024


def _cparams(sem):
    return pltpu.CompilerParams(dimension_semantics=sem, vmem_limit_bytes=VMEM_LIMIT)


def _rms(x, g):
    return x * lax.rsqrt(jnp.mean(x * x, axis=-1, keepdims=True) + EPS) * g


def _ffn_kernel(x_ref, g_ref, wg_ref, wu_ref, wd_ref, o_ref, h_ref):
    @pl.when(pl.program_id(1) == 0)
    def _():
        x = x_ref[...]
        h_ref[...] = _rms(x, g_ref[...]).astype(bf16)
        o_ref[...] = x

    h = h_ref[...]
    a = jnp.dot(h, wg_ref[...], preferred_element_type=f32)
    u = jnp.dot(h, wu_ref[...], preferred_element_type=f32)
    act = (0.5 * a * jax.nn.sigmoid(a) * u).astype(bf16)
    o_ref[...] += jnp.dot(act, wd_ref[...], preferred_element_type=f32)


def _ffn(x, g, wg, wu, wd, layer, bm, bf=512):
    m, d = x.shape
    f = wg.shape[-1]
    return pl.pallas_call(
        _ffn_kernel,
        grid=(m // bm, f // bf),
        in_specs=[
            pl.BlockSpec((bm, d), lambda i, j: (i, 0)),
            pl.BlockSpec((None, 1, d), lambda i, j: (layer, 0, 0)),
            pl.BlockSpec((None, d, bf), lambda i, j: (layer, 0, j)),
            pl.BlockSpec((None, d, bf), lambda i, j: (layer, 0, j)),
            pl.BlockSpec((None, bf, d), lambda i, j: (layer, j, 0)),
        ],
        out_specs=pl.BlockSpec((bm, d), lambda i, j: (i, 0)),
        out_shape=jax.ShapeDtypeStruct((m, d), f32),
        scratch_shapes=[pltpu.VMEM((bm, d), bf16)],
        compiler_params=_cparams(("parallel", "arbitrary")),
        name="ffn",
    )(x, g, wg, wu, wd)


def _ple_kernel(x_ref, p_ref, g_ref, wg_ref, wp_ref, gf_ref, o_ref, *, final):
    x = x_ref[...]
    h = _rms(x, g_ref[...]).astype(bf16)
    gate = jax.nn.sigmoid(jnp.dot(h, wg_ref[...], preferred_element_type=f32))
    proj = jnp.dot(p_ref[...].astype(bf16), wp_ref[...], preferred_element_type=f32)
    y = x + gate * proj
    if final:
        y = _rms(y, gf_ref[...])
    o_ref[...] = y


def _ple(x, p, g, wg, wp, gf, layer, bm, final):
    m, d = x.shape
    pd = p.shape[-1]
    return pl.pallas_call(
        functools.partial(_ple_kernel, final=final),
        grid=(m // bm,),
        in_specs=[
            pl.BlockSpec((bm, d), lambda i: (i, 0)),
            pl.BlockSpec((None, bm, pd), lambda i: (layer, i, 0)),
            pl.BlockSpec((None, 1, d), lambda i: (layer, 0, 0)),
            pl.BlockSpec((None, d, d), lambda i: (layer, 0, 0)),
            pl.BlockSpec((None, pd, d), lambda i: (layer, 0, 0)),
            pl.BlockSpec((1, d), lambda i: (0, 0)),
        ],
        out_specs=pl.BlockSpec((bm, d), lambda i: (i, 0)),
        out_shape=jax.ShapeDtypeStruct((m, d), f32),
        compiler_params=_cparams(("parallel",)),
        name="ple",
    )(x, p, g, wg, wp, gf)


def _norm_mm_kernel(x_ref, g_ref, w_ref, o_ref, h_ref):
    @pl.when(pl.program_id(1) == 0)
    def _():
        h_ref[...] = _rms(x_ref[...], g_ref[...]).astype(bf16)

    o_ref[...] = jnp.dot(h_ref[...], w_ref[...], preferred_element_type=f32)


def _norm_mm(x, g, w, layer, wlayer, bm, bn):
    m, d = x.shape
    n = w.shape[-1]
    return pl.pallas_call(
        _norm_mm_kernel,
        grid=(m // bm, n // bn),
        in_specs=[
            pl.BlockSpec((bm, d), lambda i, j: (i, 0)),
            pl.BlockSpec((None, 1, d), lambda i, j: (layer, 0, 0)),
            pl.BlockSpec((None, d, bn), lambda i, j: (wlayer, 0, j)),
        ],
        out_specs=pl.BlockSpec((bm, bn), lambda i, j: (i, j)),
        out_shape=jax.ShapeDtypeStruct((m, n), f32),
        scratch_shapes=[pltpu.VMEM((bm, d), bf16)],
        compiler_params=_cparams(("parallel", "arbitrary")),
        name="norm_mm",
    )(x, g, w)


def _mm_res_kernel(a_ref, w_ref, x_ref, o_ref):
    o_ref[...] = x_ref[...] + jnp.dot(a_ref[...], w_ref[...], preferred_element_type=f32)


def _mm_res(a, w, x, wlayer, bm, bn):
    m, k = a.shape
    n = w.shape[-1]
    return pl.pallas_call(
        _mm_res_kernel,
        grid=(m // bm, n // bn),
        in_specs=[
            pl.BlockSpec((bm, k), lambda i, j: (i, 0)),
            pl.BlockSpec((None, k, bn), lambda i, j: (wlayer, 0, j)),
            pl.BlockSpec((bm, bn), lambda i, j: (i, j)),
        ],
        out_specs=pl.BlockSpec((bm, bn), lambda i, j: (i, j)),
        out_shape=jax.ShapeDtypeStruct((m, n), f32),
        compiler_params=_cparams(("parallel", "arbitrary")),
        name="mm_res",
    )(a, w, x)


def _split3(x):
    hi = x.astype(bf16)
    r1 = x - hi.astype(f32)
    mid = r1.astype(bf16)
    lo = (r1 - mid.astype(f32)).astype(bf16)
    return hi, mid, lo


def _gla_kernel(q_ref, k_ref, v_ref, r_ref, a1_ref, wa2_ref, ba_ref, gn_ref, s0_ref,
                og_ref, sout_ref, s_ref, *, chunk, n_chunks, valid, qscale):
    c_rows = chunk

    @pl.when(pl.program_id(2) == 0)
    def _():
        s_ref[...] = s0_ref[0, 0]

    row = lax.broadcasted_iota(jnp.int32, (c_rows, c_rows), 0)
    col = lax.broadcasted_iota(jnp.int32, (c_rows, c_rows), 1)
    tril = col <= row
    tri_b = tril.astype(bf16)
    ones_b = jnp.ones((c_rows, 128), bf16)
    wa2 = wa2_ref[...]
    ba = ba_ref[...]
    gn = gn_ref[...]
    tn = (((0,), (0,)), ((), ()))
    nt = (((1,), (1,)), ((), ()))

    def body(c, carry):
        sl = pl.ds(pl.multiple_of(c * c_rows, c_rows), c_rows)
        x = jnp.dot(a1_ref[0, sl, :].astype(bf16), wa2, preferred_element_type=f32) + ba
        la = (jnp.minimum(x, 0.0) - jnp.log1p(jnp.exp(-jnp.abs(x)))) * (1.0 / GLA_TAU)
        if valid < c_rows:
            la = jnp.where(lax.broadcasted_iota(jnp.int32, la.shape, 0) < valid, la, 0.0)
        parts = _split3(la)
        b = sum(jnp.dot(tri_b, p, preferred_element_type=f32) for p in parts)
        dl_col = sum(lax.dot_general(p, ones_b, tn, preferred_element_type=f32) for p in parts)
        b_last = b[c_rows - 1:c_rows, :]
        q = q_ref[0, sl, :] * qscale
        k = k_ref[0, sl, :]
        vb = v_ref[0, sl, :].astype(bf16)
        qd = (q * jnp.exp(b)).astype(bf16)
        ki = (k * jnp.exp(-b)).astype(bf16)
        ke = (k * jnp.exp(b_last - b)).astype(bf16)
        att = lax.dot_general(qd, ki, nt, preferred_element_type=f32)
        att = jnp.where(tril, att, 0.0).astype(bf16)
        s_old = s_ref[...]
        o = (jnp.dot(att, vb, preferred_element_type=f32)
             + jnp.dot(qd, s_old.astype(bf16), preferred_element_type=f32))
        decay = jnp.exp(dl_col)
        dv = s_old.shape[1]
        decay = jnp.concatenate([decay] * (dv // 128), axis=1)
        s_ref[...] = s_old * decay + lax.dot_general(ke, vb, tn, preferred_element_type=f32)
        o = o * lax.rsqrt(jnp.mean(o * o, axis=-1, keepdims=True) + EPS) * gn
        r = r_ref[0, sl, :]
        og_ref[0, sl, :] = (o * (r * jax.nn.sigmoid(r))).astype(og_ref.dtype)
        return carry

    lax.fori_loop(0, n_chunks, body, 0)

    @pl.when(pl.program_id(2) == pl.num_programs(2) - 1)
    def _():
        sout_ref[0, 0] = s_ref[...]


def _gla(proj, wa2, ba, gn, s0, layer, tb, valid):
    bsz, t, _ = proj.shape
    heads = GLA_HEADS
    dk, dv = s0.shape[-2], s0.shape[-1]
    kd, vd = heads * dk, heads * dv
    a1_blk = (2 * kd + 2 * vd) // GLA_RANK_PAD
    kern = functools.partial(_gla_kernel, chunk=GLA_CHUNK, n_chunks=tb // GLA_CHUNK, valid=valid,
                             qscale=float(dk) ** -0.5)
    return pl.pallas_call(
        kern,
        grid=(bsz, heads, t // tb),
        in_specs=[
            pl.BlockSpec((1, tb, dk), lambda b, h, i: (b, i, h)),
            pl.BlockSpec((1, tb, dk), lambda b, h, i: (b, i, heads + h)),
            pl.BlockSpec((1, tb, dv), lambda b, h, i: (b, i, 2 * kd // dv + h)),
            pl.BlockSpec((1, tb, dv), lambda b, h, i: (b, i, (2 * kd + vd) // dv + h)),
            pl.BlockSpec((1, tb, GLA_RANK_PAD), lambda b, h, i: (b, i, a1_blk)),
            pl.BlockSpec((None, GLA_RANK_PAD, dk), lambda b, h, i: (layer, 0, h)),
            pl.BlockSpec((None, 1, dk), lambda b, h, i: (layer, 0, h)),
            pl.BlockSpec((None, 1, dv), lambda b, h, i: (layer, 0, h)),
            pl.BlockSpec((1, 1, dk, dv), lambda b, h, i: (b, h, 0, 0)),
        ],
        out_specs=[
            pl.BlockSpec((1, tb, dv), lambda b, h, i: (b, i, h)),
            pl.BlockSpec((1, 1, dk, dv), lambda b, h, i: (b, h, 0, 0)),
        ],
        out_shape=[
            jax.ShapeDtypeStruct((bsz, t, vd), bf16),
            jax.ShapeDtypeStruct((bsz, heads, dk, dv), f32),
        ],
        scratch_shapes=[pltpu.VMEM((dk, dv), f32)],
        compiler_params=_cparams(("parallel", "parallel", "arbitrary")),
        name="gla",
    )(proj, proj, proj, proj, proj, wa2, ba, gn, s0)


def _softplus_both(z):
    sp = jnp.maximum(z, 0.0) + jnp.log(1.0 + jnp.exp(-jnp.abs(z)))
    return sp, sp - z


def _sbp_kernel(q_ref, k_ref, v_ref, bias_ref, o_ref, *, bq, scale):
    i = pl.program_id(2)
    q = (q_ref[0] * scale).astype(bf16)
    hd = q.shape[-1]
    bias = bias_ref[0]
    sub = SB_SUB
    rj = lax.broadcasted_iota(jnp.int32, (sub, 2 * sub), 0)
    cs = lax.broadcasted_iota(jnp.int32, (sub, 2 * sub), 1)
    ntri = jnp.where((cs >= sub) | (rj > cs), -1.0, 0.0).astype(bf16)
    col_minus_row = (lax.broadcasted_iota(jnp.int32, (bq, SB_KEYS), 1)
                     - lax.broadcasted_iota(jnp.int32, (bq, SB_KEYS), 0))
    nt = (((1,), (1,)), ((), ()))
    steps_per_q = bq // SB_KEYS

    def step(kb, carry, acc, mask_offset):
        ks = pl.ds(pl.multiple_of(kb * SB_KEYS, SB_KEYS), SB_KEYS)
        k = k_ref[0, ks, :].astype(bf16)
        v = v_ref[0, ks, :].astype(bf16)
        z = lax.dot_general(q, k, nt, preferred_element_type=f32) + bias
        sp, spn = _softplus_both(z)
        if mask_offset is not None:
            past = col_minus_row < mask_offset
            sp = jnp.where(past, sp, 0.0)
        spb = sp.astype(bf16)
        r_hi = jnp.dot(spb[:, sub:], ntri, preferred_element_type=f32)
        r_lo = jnp.dot(spb[:, :sub], ntri, preferred_element_type=f32)
        c_mid = carry + r_hi[:, sub:]
        after = jnp.concatenate([c_mid + r_lo[:, :sub], carry + r_hi[:, :sub]], axis=1)
        a = jnp.exp(after - spn)
        if mask_offset is not None:
            a = jnp.where(past, a, 0.0)
        acc = acc + jnp.dot(a.astype(bf16), v, preferred_element_type=f32)
        return c_mid + r_lo[:, sub:], acc

    carry = jnp.zeros((bq, sub), f32)
    acc = jnp.zeros((bq, hd), f32)
    for d in range(steps_per_q - 1, -1, -1):
        carry, acc = step(i * steps_per_q + d, carry, acc, -d * SB_KEYS)

    def body(n, ca):
        kb = i * steps_per_q - 1 - 2 * n
        c, a = step(kb, ca[0], ca[1], None)
        return step(kb - 1, c, a, None)

    carry, acc = lax.fori_loop(0, i * (steps_per_q // 2), body, (carry, acc))
    o_ref[0] = acc.astype(o_ref.dtype)


def _sb_prompt(qkv, bias, heads, bq=512):
    bsz, t, d3 = qkv.shape
    d = d3 // 3
    hd = d // heads
    assert bq % (2 * SB_KEYS) == 0 and t % bq == 0
    kern = functools.partial(_sbp_kernel, bq=bq, scale=float(hd) ** -0.5)
    return pl.pallas_call(
        kern,
        grid=(bsz, heads, t // bq),
        in_specs=[
            pl.BlockSpec((1, bq, hd), lambda b, h, i: (b, i, h)),
            pl.BlockSpec((1, t, hd), lambda b, h, i: (b, 0, heads + h)),
            pl.BlockSpec((1, t, hd), lambda b, h, i: (b, 0, 2 * heads + h)),
            pl.BlockSpec((1, 1, SB_KEYS), lambda b, h, i: (h, 0, 0)),
        ],
        out_specs=pl.BlockSpec((1, bq, hd), lambda b, h, i: (b, i, h)),
        out_shape=jax.ShapeDtypeStruct((bsz, t, d), bf16),
        compiler_params=_cparams(("parallel", "parallel", "arbitrary")),
        name="sb_prompt",
    )(qkv, qkv, qkv, bias)


def _sbs_kernel(pt_ref, q_ref, kn_ref, vn_ref, kca_ref, vca_ref, kcb_ref, vcb_ref, bias_ref, o_ref,
                qm_ref, acc_ref, carry_ref, *, heads, nq, scale):
    p = pl.program_id(1)
    hd = kca_ref.shape[2]
    ps = kca_ref.shape[1] // heads
    d = heads * hd
    nc = heads * nq
    bias = bias_ref[...]
    rs = lax.broadcasted_iota(jnp.int32, (2 * ps, 2 * ps), 0)
    cj = lax.broadcasted_iota(jnp.int32, (2 * ps, 2 * ps), 1)
    tri = (cj > rs).astype(bf16)
    nt = (((1,), (1,)), ((), ()))
    tn = (((0,), (0,)), ((), ()))

    def gather(ref):
        return jnp.concatenate(
            [ref[0, pl.ds(h, ps, stride=heads), :].astype(bf16) for h in range(heads)], axis=1)

    def attend(kp, vp, tri_m, masked):
        n = kp.shape[0]
        z = lax.dot_general(kp, qm_ref[...], nt, preferred_element_type=f32) * scale + bias
        sp, spn = _softplus_both(z)
        if masked:
            key = lax.broadcasted_iota(jnp.int32, (n, nc), 0)
            qry = lax.broadcasted_iota(jnp.int32, (n, nc), 1) % nq
            past = key < qry
            sp = jnp.where(past, sp, 0.0)
        after = carry_ref[0:1, :] - jnp.dot(tri_m, sp.astype(bf16), preferred_element_type=f32)
        w = jnp.exp(after - spn)
        if masked:
            w = jnp.where(past, w, 0.0)
        acc_ref[...] += lax.dot_general(w.astype(bf16), vp, tn, preferred_element_type=f32)
        carry_ref[...] = carry_ref[...] - jnp.sum(sp, axis=0, keepdims=True)

    @pl.when(p == 0)
    def _():
        q = q_ref[0]
        qrep = jnp.concatenate([q] * heads, axis=0)
        rh = lax.broadcasted_iota(jnp.int32, (nc, d), 0) // nq
        ch = lax.broadcasted_iota(jnp.int32, (nc, d), 1) // hd
        qm_ref[...] = jnp.where(rh == ch, qrep, 0.0).astype(bf16)
        acc_ref[...] = jnp.zeros_like(acc_ref)
        carry_ref[...] = jnp.zeros_like(carry_ref)
        attend(kn_ref[0].astype(bf16), vn_ref[0].astype(bf16), tri[:ps, :ps], True)

    attend(jnp.concatenate([gather(kcb_ref), gather(kca_ref)], axis=0),
           jnp.concatenate([gather(vcb_ref), gather(vca_ref)], axis=0), tri, False)

    @pl.when(p == pl.num_programs(1) - 1)
    def _():
        for h in range(heads):
            o_ref[0, :, h * hd:(h + 1) * hd] = acc_ref[h * nq:(h + 1) * nq, h * hd:(h + 1) * hd].astype(o_ref.dtype)


def _sb_sample(q, k_new, v_new, cache_k, cache_v, page_table, bias, layer, heads):
    bsz, nq, d = q.shape
    n_pages = page_table.shape[1]
    assert n_pages % 2 == 0
    hd = d // heads
    ps = cache_k.shape[2] // heads
    nc = heads * nq
    kern = functools.partial(_sbs_kernel, heads=heads, nq=nq, scale=float(hd) ** -0.5)
    cache_spec = lambda back: pl.BlockSpec(
        (None, 1, ps * heads, hd), lambda b, p, pt: (layer, pt[b, n_pages - back - 2 * p], 0, 0))
    grid_spec = pltpu.PrefetchScalarGridSpec(
        num_scalar_prefetch=1,
        grid=(bsz, n_pages // 2),
        in_specs=[
            pl.BlockSpec((1, nq, d), lambda b, p, pt: (b, 0, 0)),
            pl.BlockSpec((1, ps, d), lambda b, p, pt: (b, 0, 0)),
            pl.BlockSpec((1, ps, d), lambda b, p, pt: (b, 0, 0)),
            cache_spec(1), cache_spec(1), cache_spec(2), cache_spec(2),
            pl.BlockSpec((1, nc), lambda b, p, pt: (0, 0)),
        ],
        out_specs=pl.BlockSpec((1, nq, d), lambda b, p, pt: (b, 0, 0)),
        scratch_shapes=[
            pltpu.VMEM((nc, d), bf16),
            pltpu.VMEM((nc, d), f32),
            pltpu.VMEM((8, nc), f32),
        ],
    )
    return pl.pallas_call(
        kern,
        grid_spec=grid_spec,
        out_shape=jax.ShapeDtypeStruct((bsz, nq, d), bf16),
        compiler_params=_cparams(("parallel", "arbitrary")),
        name="sb_sample",
    )(page_table, q, k_new, v_new, cache_k, cache_v, cache_k, cache_v, bias)


def _pool_kernel(*refs, bm, has_halo, pos0, windows):
    if has_halo:
        x_ref, xh_ref, st_ref, g_ref, w_ref, sc_ref, o_ref, so_ref, ext_ref = refs
    else:
        x_ref, st_ref, g_ref, w_ref, sc_ref, o_ref, so_ref, ext_ref = refs
    i = pl.program_id(1)
    g = g_ref[...]
    x = x_ref[0]
    h = _rms(x, g)
    if has_halo:
        halo = jnp.where(i == 0, st_ref[0], _rms(xh_ref[0], g))
    else:
        halo = st_ref[0]
    ext_ref[0:POOL_HALO, :] = halo
    ext_ref[POOL_HALO:POOL_HALO + bm, :] = h
    gd = x.shape[-1] // len(windows)
    pos = pos0 + i * bm + lax.broadcasted_iota(jnp.int32, (bm, 1), 0)
    for gi, w in enumerate(windows):
        cs = slice(gi * gd, (gi + 1) * gd)
        hg = h[:, cs]
        win = hg
        for s in range(1, w):
            win = win + ext_ref[pl.ds(POOL_HALO - s, bm), cs]
        cnt = jnp.minimum(w, pos + 1).astype(f32)
        dlt = (win / cnt - hg).astype(bf16)
        y = jnp.dot(dlt, w_ref[gi], preferred_element_type=f32)
        o_ref[0, :, cs] = x[:, cs] + y * sc_ref[:, cs]
    so_ref[0] = ext_ref[pl.ds(bm, POOL_HALO), :]


def _pool(x, state, g, w, sc, layer, wlayer, bm, pos0):
    bsz, t, d = x.shape
    ng, gd = w.shape[1], w.shape[2]
    has_halo = t > bm
    hb = bm // POOL_HALO
    kern = functools.partial(_pool_kernel, bm=bm, has_halo=has_halo, pos0=pos0, windows=POOL_WINDOWS)
    in_specs = [pl.BlockSpec((1, bm, d), lambda b, i: (b, i, 0))]
    args = [x]
    if has_halo:
        in_specs.append(pl.BlockSpec((1, POOL_HALO, d), lambda b, i: (b, jnp.maximum(i * hb - 1, 0), 0)))
        args.append(x)
    in_specs += [
        pl.BlockSpec((1, POOL_HALO, d), lambda b, i: (b, 0, 0)),
        pl.BlockSpec((None, 1, d), lambda b, i: (layer, 0, 0)),
        pl.BlockSpec((None, ng, gd, gd), lambda b, i: (wlayer, 0, 0, 0)),
        pl.BlockSpec((None, 1, d), lambda b, i: (wlayer, 0, 0)),
    ]
    args += [state, g, w, sc]
    return pl.pallas_call(
        kern,
        grid=(bsz, t // bm),
        in_specs=in_specs,
        out_specs=[
            pl.BlockSpec((1, bm, d), lambda b, i: (b, i, 0)),
            pl.BlockSpec((1, POOL_HALO, d), lambda b, i: (b, 0, 0)),
        ],
        out_shape=[
            jax.ShapeDtypeStruct((bsz, t, d), f32),
            jax.ShapeDtypeStruct((bsz, POOL_HALO, d), f32),
        ],
        scratch_shapes=[pltpu.VMEM((POOL_HALO + bm, d), f32)],
        compiler_params=_cparams(("parallel", "arbitrary")),
        name="pool",
    )(*args)


def kernel(x_prompt, x_sample, state_gla, cache_sb_k, cache_sb_v, state_pool, page_table, p_prompt, p_sample, norm_ffn1, ffn1_gate, ffn1_up, ffn1_down, norm_mix, norm_ffn2, ffn2_gate, ffn2_up, ffn2_down, norm_ple, ple_gate, ple_proj, gla_in, gla_a1, gla_a2, gla_a_bias, gla_norm, gla_out, sb_in, sb_bias, sb_out, pool_w, pool_scale, norm_final):
    bp, t, d = x_prompt.shape
    bs, ts, _ = x_sample.shape
    depth = norm_ffn1.shape[0]
    mp, ms = bp * t, bs * ts
    kd = gla_a2.shape[-1]
    dk = kd // GLA_HEADS
    dv = gla_out.shape[1] // GLA_HEADS
    hd = d // SB_HEADS
    page = cache_sb_k.shape[2]
    pstate = state_pool.shape[2]

    row3 = lambda a: a.reshape(a.shape[0], 1, a.shape[-1])
    n_ffn1, n_mix, n_ffn2, n_ple = row3(norm_ffn1), row3(norm_mix), row3(norm_ffn2), row3(norm_ple)
    n_final = norm_final.reshape(1, d)
    w1g, w1u, w1d = ffn1_gate.astype(bf16), ffn1_up.astype(bf16), ffn1_down.astype(bf16)
    w2g, w2u, w2d = ffn2_gate.astype(bf16), ffn2_up.astype(bf16), ffn2_down.astype(bf16)
    wpg, wpp = ple_gate.astype(bf16), ple_proj.astype(bf16)
    n_gla = gla_in.shape[0]
    gla_w = jnp.concatenate(
        [gla_in, gla_a1, jnp.zeros((n_gla, d, GLA_RANK_PAD - gla_a1.shape[-1]), f32)], axis=-1).astype(bf16)
    gla_wa2 = jnp.concatenate(
        [gla_a2, jnp.zeros((n_gla, GLA_RANK_PAD - gla_a2.shape[1], kd), f32)], axis=1).astype(bf16)
    gla_ba = row3(gla_a_bias)
    gla_gn = gla_norm.reshape(n_gla, 1, GLA_HEADS * dv)
    gla_wo = gla_out.astype(bf16)
    sb_wi, sb_wo = sb_in.astype(bf16), sb_out.astype(bf16)
    pool_wb = pool_w.astype(bf16)
    pool_sc = row3(pool_scale)
    pp = p_prompt.reshape(depth, mp, -1)
    psm = p_sample.reshape(depth, ms, -1)

    xp = x_prompt.reshape(mp, d)
    xs = x_sample.reshape(ms, d)
    gla_p, gla_s, kp, vp, ksm, vsm, pool_p, pool_s = [], [], [], [], [], [], [], []
    for i in range(depth):
        kind, j = i % N_MIXERS, i // N_MIXERS
        xp = _ffn(xp, n_ffn1, w1g, w1u, w1d, i, bm=512)
        xs = _ffn(xs, n_ffn1, w1g, w1u, w1d, i, bm=ms)
        if kind == 0:
            proj_p = _norm_mm(xp, n_mix, gla_w, i, j, bm=512, bn=896).reshape(bp, t, -1)
            proj_s = _norm_mm(xs, n_mix, gla_w, i, j, bm=ms, bn=896).reshape(bs, ts, -1)
            proj_s = jnp.pad(proj_s, ((0, 0), (0, GLA_CHUNK - ts), (0, 0)))
            s0 = jnp.zeros((bp, GLA_HEADS, dk, dv), f32)
            og_p, st_p = _gla(proj_p, gla_wa2, gla_ba, gla_gn, s0, j, tb=512, valid=GLA_CHUNK)
            og_s, st_s = _gla(proj_s, gla_wa2, gla_ba, gla_gn, state_gla[j], j, tb=GLA_CHUNK, valid=ts)
            xp = _mm_res(og_p.reshape(mp, -1), gla_wo, xp, j, bm=512, bn=512)
            xs = _mm_res(og_s[:, :ts].reshape(ms, -1), gla_wo, xs, j, bm=ms, bn=512)
            gla_p.append(st_p)
            gla_s.append(st_s)
        elif kind == 1:
            qkv_p = _norm_mm(xp, n_mix, sb_wi, i, j, bm=512, bn=512)
            qkv_s = _norm_mm(xs, n_mix, sb_wi, i, j, bm=ms, bn=512)
            bias_p = jnp.broadcast_to(sb_bias[j][:, None, None], (SB_HEADS, 1, SB_KEYS))
            o_p = _sb_prompt(qkv_p.reshape(bp, t, 3 * d), bias_p, SB_HEADS)
            k_s = qkv_s[:, d:2 * d].reshape(bs, ts, d)
            v_s = qkv_s[:, 2 * d:].reshape(bs, ts, d)
            pad = ((0, 0), (0, page - ts), (0, 0))
            bias_s = jnp.repeat(sb_bias[j], ts)[None, :]
            o_s = _sb_sample(qkv_s[:, :d].reshape(bs, ts, d), jnp.pad(k_s, pad), jnp.pad(v_s, pad),
                             cache_sb_k.reshape(cache_sb_k.shape[:2] + (page * SB_HEADS, hd)),
                             cache_sb_v.reshape(cache_sb_v.shape[:2] + (page * SB_HEADS, hd)),
                             page_table, bias_s, j, SB_HEADS)
            xp = _mm_res(o_p.reshape(mp, d), sb_wo, xp, j, bm=512, bn=512)
            xs = _mm_res(o_s.reshape(ms, d), sb_wo, xs, j, bm=ms, bn=512)
            kp.append(qkv_p[:, d:2 * d].reshape(bp, t, SB_HEADS, hd))
            vp.append(qkv_p[:, 2 * d:].reshape(bp, t, SB_HEADS, hd))
            ksm.append(k_s.reshape(bs, ts, SB_HEADS, hd))
            vsm.append(v_s.reshape(bs, ts, SB_HEADS, hd))
        else:
            st0_p = jnp.zeros((bp, POOL_HALO, d), f32)
            st0_s = jnp.pad(state_pool[j], ((0, 0), (POOL_HALO - pstate, 0), (0, 0)))
            xp3, so_p = _pool(xp.reshape(bp, t, d), st0_p, n_mix, pool_wb, pool_sc, i, j, bm=512, pos0=0)
            xs3, so_s = _pool(xs.reshape(bs, ts, d), st0_s, n_mix, pool_wb, pool_sc, i, j, bm=ts,
                              pos0=page_table.shape[1] * page)
            xp, xs = xp3.reshape(mp, d), xs3.reshape(ms, d)
            pool_p.append(so_p[:, POOL_HALO - pstate:])
            pool_s.append(so_s[:, POOL_HALO - pstate:])
        xp = _ffn(xp, n_ffn2, w2g, w2u, w2d, i, bm=512)
        xs = _ffn(xs, n_ffn2, w2g, w2u, w2d, i, bm=ms)
        final = i == depth - 1
        xp = _ple(xp, pp, n_ple, wpg, wpp, n_final, i, bm=256, final=final)
        xs = _ple(xs, psm, n_ple, wpg, wpp, n_final, i, bm=ms, final=final)
    return (xp.reshape(bp, t, d), xs.reshape(bs, ts, d), jnp.stack(gla_p), jnp.stack(gla_s),
            jnp.stack(kp), jnp.stack(vp), jnp.stack(ksm), jnp.stack(vsm),
            jnp.stack(pool_p), jnp.stack(pool_s))
```

```python
import functools

import jax
import jax.numpy as jnp
from jax import lax
from jax.experimental import pallas as pl
from jax.experimental.pallas import tpu as pltpu

f32 = jnp.float32
bf16 = jnp.bfloat16

EPS = 1e-6
N_MIXERS = 3
GLA_HEADS = 4
GLA_TAU = 16.0
GLA_CHUNK = 64
GLA_RANK_PAD = 128
SB_HEADS = 16
SB_KEYS = 256
SB_SUB = 128
POOL_WINDOWS = (2, 4, 8, 16)
POOL_HALO = 16
VMEM_LIMIT = 58 * 1024 * 1024
FFN_ROWS = 1024


def _cparams(sem):
    return pltpu.CompilerParams(dimension_semantics=sem, vmem_limit_bytes=VMEM_LIMIT)


def _rms(x, g):
    return x * lax.rsqrt(jnp.mean(x * x, axis=-1, keepdims=True) + EPS) * g


def _ffn_kernel(x_ref, g_ref, wg_ref, wu_ref, wd_ref, o_ref, h_ref):
    @pl.when(pl.program_id(1) == 0)
    def _():
        x = x_ref[...]
        h_ref[...] = _rms(x, g_ref[...]).astype(bf16)
        o_ref[...] = x

    h = h_ref[...]
    a = jnp.dot(h, wg_ref[...], preferred_element_type=f32)
    u = jnp.dot(h, wu_ref[...], preferred_element_type=f32)
    act = (0.5 * a * jax.nn.sigmoid(a) * u).astype(bf16)
    o_ref[...] += jnp.dot(act, wd_ref[...], preferred_element_type=f32)


def _ffn(x, g, wg, wu, wd, layer, bm, bf=512):
    m, d = x.shape
    f = wg.shape[-1]
    return pl.pallas_call(
        _ffn_kernel,
        grid=(m // bm, f // bf),
        in_specs=[
            pl.BlockSpec((bm, d), lambda i, j: (i, 0), pipeline_mode=pl.Buffered(1)),
            pl.BlockSpec((None, 1, d), lambda i, j: (layer, 0, 0)),
            pl.BlockSpec((None, d, bf), lambda i, j: (layer, 0, j)),
            pl.BlockSpec((None, d, bf), lambda i, j: (layer, 0, j)),
            pl.BlockSpec((None, bf, d), lambda i, j: (layer, j, 0)),
        ],
        out_specs=pl.BlockSpec((bm, d), lambda i, j: (i, 0)),
        out_shape=jax.ShapeDtypeStruct((m, d), f32),
        scratch_shapes=[pltpu.VMEM((bm, d), bf16)],
        compiler_params=_cparams(("parallel", "arbitrary")),
        name="ffn",
    )(x, g, wg, wu, wd)


def _ple_kernel(x_ref, p_ref, g_ref, wg_ref, wp_ref, gf_ref, o_ref, *, final):
    x = x_ref[...]
    h = _rms(x, g_ref[...]).astype(bf16)
    gate = jax.nn.sigmoid(jnp.dot(h, wg_ref[...], preferred_element_type=f32))
    proj = jnp.dot(p_ref[...].astype(bf16), wp_ref[...], preferred_element_type=f32)
    y = x + gate * proj
    if final:
        y = _rms(y, gf_ref[...])
    o_ref[...] = y


def _ple(x, p, g, wg, wp, gf, layer, bm, final):
    m, d = x.shape
    pd = p.shape[-1]
    return pl.pallas_call(
        functools.partial(_ple_kernel, final=final),
        grid=(m // bm,),
        in_specs=[
            pl.BlockSpec((bm, d), lambda i: (i, 0)),
            pl.BlockSpec((None, bm, pd), lambda i: (layer, i, 0)),
            pl.BlockSpec((None, 1, d), lambda i: (layer, 0, 0)),
            pl.BlockSpec((None, d, d), lambda i: (layer, 0, 0), pipeline_mode=pl.Buffered(1)),
            pl.BlockSpec((None, pd, d), lambda i: (layer, 0, 0), pipeline_mode=pl.Buffered(1)),
            pl.BlockSpec((1, d), lambda i: (0, 0)),
        ],
        out_specs=pl.BlockSpec((bm, d), lambda i: (i, 0)),
        out_shape=jax.ShapeDtypeStruct((m, d), f32),
        compiler_params=_cparams(("parallel",)),
        name="ple",
    )(x, p, g, wg, wp, gf)


def _norm_mm_kernel(*refs, has_aux):
    if has_aux:
        x_ref, g_ref, w_ref, w2_ref, o_ref, o2_ref, h_ref = refs
    else:
        x_ref, g_ref, w_ref, o_ref, h_ref = refs

    @pl.when(pl.program_id(1) == 0)
    def _():
        h_ref[...] = _rms(x_ref[...], g_ref[...]).astype(bf16)
        if has_aux:
            o2_ref[...] = jnp.dot(h_ref[...], w2_ref[...], preferred_element_type=f32)

    o_ref[...] = jnp.dot(h_ref[...], w_ref[...], preferred_element_type=f32)


def _norm_mm(x, g, w, layer, wlayer, bm, bn, col0=0, n=None, w2=None):
    m, d = x.shape
    n = w.shape[-1] if n is None else n
    cb = col0 // bn
    assert col0 % bn == 0 and n % bn == 0 and m % bm == 0
    in_specs = [
        pl.BlockSpec((bm, d), lambda i, j: (i, 0)),
        pl.BlockSpec((None, 1, d), lambda i, j: (layer, 0, 0)),
        pl.BlockSpec((None, d, bn), lambda i, j: (wlayer, 0, cb + j)),
    ]
    out_specs = [pl.BlockSpec((bm, bn), lambda i, j: (i, j))]
    out_shape = [jax.ShapeDtypeStruct((m, n), f32)]
    args = [x, g, w]
    if w2 is not None:
        n2 = w2.shape[-1]
        in_specs.append(pl.BlockSpec((None, d, n2), lambda i, j: (wlayer, 0, 0)))
        out_specs.append(pl.BlockSpec((bm, n2), lambda i, j: (i, 0)))
        out_shape.append(jax.ShapeDtypeStruct((m, n2), f32))
        args.append(w2)
    outs = pl.pallas_call(
        functools.partial(_norm_mm_kernel, has_aux=w2 is not None),
        grid=(m // bm, n // bn),
        in_specs=in_specs,
        out_specs=out_specs,
        out_shape=out_shape,
        scratch_shapes=[pltpu.VMEM((bm, d), bf16)],
        compiler_params=_cparams(("parallel", "arbitrary")),
        name="norm_mm",
    )(*args)
    return outs if w2 is not None else outs[0]


def _mm_res_kernel(a_ref, w_ref, x_ref, o_ref):
    o_ref[...] = x_ref[...] + jnp.dot(a_ref[...], w_ref[...], preferred_element_type=f32)


def _mm_res(a, w, x, wlayer, bm, bn):
    m, k = a.shape
    n = w.shape[-1]
    return pl.pallas_call(
        _mm_res_kernel,
        grid=(m // bm, n // bn),
        in_specs=[
            pl.BlockSpec((bm, k), lambda i, j: (i, 0)),
            pl.BlockSpec((None, k, bn), lambda i, j: (wlayer, 0, j)),
            pl.BlockSpec((bm, bn), lambda i, j: (i, j)),
        ],
        out_specs=pl.BlockSpec((bm, bn), lambda i, j: (i, j)),
        out_shape=jax.ShapeDtypeStruct((m, n), f32),
        compiler_params=_cparams(("parallel", "arbitrary")),
        name="mm_res",
    )(a, w, x)


def _split3(x):
    hi = x.astype(bf16)
    r1 = x - hi.astype(f32)
    mid = r1.astype(bf16)
    lo = (r1 - mid.astype(f32)).astype(bf16)
    return hi, mid, lo


def _gla_kernel(q_ref, k_ref, v_ref, r_ref, a1_ref, wa2_ref, ba_ref, gn_ref, s0_ref,
                og_ref, sout_ref, s_ref, *, chunk, n_chunks, valid, qscale):
    c_rows = chunk
    heads, dk, dv = s_ref.shape

    @pl.when(pl.program_id(1) == 0)
    def _():
        s_ref[...] = s0_ref[0]

    row = lax.broadcasted_iota(jnp.int32, (c_rows, c_rows), 0)
    col = lax.broadcasted_iota(jnp.int32, (c_rows, c_rows), 1)
    tril = col <= row
    tri_b = tril.astype(bf16)
    ones_b = jnp.ones((c_rows, 128), bf16)
    wa2 = wa2_ref[...]
    ba = ba_ref[...]
    gn = gn_ref[...]
    tn = (((0,), (0,)), ((), ()))
    nt = (((1,), (1,)), ((), ()))

    def body(c, carry):
        sl = pl.ds(pl.multiple_of(c * c_rows, c_rows), c_rows)
        x = jnp.dot(a1_ref[0, sl, :].astype(bf16), wa2, preferred_element_type=f32) + ba
        la = (jnp.minimum(x, 0.0) - jnp.log1p(jnp.exp(-jnp.abs(x)))) * (1.0 / GLA_TAU)
        if valid < c_rows:
            la = jnp.where(lax.broadcasted_iota(jnp.int32, la.shape, 0) < valid, la, 0.0)
        parts = _split3(la)
        b = sum(jnp.dot(tri_b, p, preferred_element_type=f32) for p in parts)
        b_last = b[c_rows - 1:c_rows, :]
        q = q_ref[0, sl, :] * qscale
        k = k_ref[0, sl, :]
        vb = v_ref[0, sl, :].astype(bf16)
        r = r_ref[0, sl, :]
        qd = (q * jnp.exp(b)).astype(bf16)
        ki = (k * jnp.exp(-b)).astype(bf16)
        ke = (k * jnp.exp(b_last - b)).astype(bf16)
        for h in range(heads):
            kc = slice(h * dk, (h + 1) * dk)
            vc = slice(h * dv, (h + 1) * dv)
            att = lax.dot_general(qd[:, kc], ki[:, kc], nt, preferred_element_type=f32)
            att = jnp.where(tril, att, 0.0).astype(bf16)
            s_old = s_ref[h]
            o = (jnp.dot(att, vb[:, vc], preferred_element_type=f32)
                 + jnp.dot(qd[:, kc], s_old.astype(bf16), preferred_element_type=f32))
            dl_col = sum(lax.dot_general(p[:, kc], ones_b, tn, preferred_element_type=f32) for p in parts)
            decay = jnp.concatenate([jnp.exp(dl_col)] * (dv // 128), axis=1)
            s_ref[h] = s_old * decay + lax.dot_general(ke[:, kc], vb[:, vc], tn, preferred_element_type=f32)
            o = o * lax.rsqrt(jnp.mean(o * o, axis=-1, keepdims=True) + EPS) * gn[:, vc]
            rh = r[:, vc]
            og_ref[0, sl, vc] = (o * (rh * jax.nn.sigmoid(rh))).astype(og_ref.dtype)
        return carry

    lax.fori_loop(0, n_chunks, body, 0)

    @pl.when(pl.program_id(1) == pl.num_programs(1) - 1)
    def _():
        sout_ref[0] = s_ref[...]


def _gla(proj, a1, wa2, ba, gn, s0, layer, tb, valid):
    bsz, t, _ = proj.shape
    heads = GLA_HEADS
    dk, dv = s0.shape[-2], s0.shape[-1]
    kd, vd = heads * dk, heads * dv
    assert vd == 2 * kd
    kern = functools.partial(_gla_kernel, chunk=GLA_CHUNK, n_chunks=tb // GLA_CHUNK, valid=valid,
                             qscale=float(dk) ** -0.5)
    return pl.pallas_call(
        kern,
        grid=(bsz, t // tb),
        in_specs=[
            pl.BlockSpec((1, tb, kd), lambda b, i: (b, i, 0)),
            pl.BlockSpec((1, tb, kd), lambda b, i: (b, i, 1)),
            pl.BlockSpec((1, tb, vd), lambda b, i: (b, i, 1)),
            pl.BlockSpec((1, tb, vd), lambda b, i: (b, i, 2)),
            pl.BlockSpec((1, tb, GLA_RANK_PAD), lambda b, i: (b, i, 0)),
            pl.BlockSpec((None, GLA_RANK_PAD, kd), lambda b, i: (layer, 0, 0)),
            pl.BlockSpec((None, 1, kd), lambda b, i: (layer, 0, 0)),
            pl.BlockSpec((None, 1, vd), lambda b, i: (layer, 0, 0)),
            pl.BlockSpec((1, heads, dk, dv), lambda b, i: (b, 0, 0, 0)),
        ],
        out_specs=[
            pl.BlockSpec((1, tb, vd), lambda b, i: (b, i, 0)),
            pl.BlockSpec((1, heads, dk, dv), lambda b, i: (b, 0, 0, 0)),
        ],
        out_shape=[
            jax.ShapeDtypeStruct((bsz, t, vd), bf16),
            jax.ShapeDtypeStruct((bsz, heads, dk, dv), f32),
        ],
        scratch_shapes=[pltpu.VMEM((heads, dk, dv), f32)],
        compiler_params=_cparams(("parallel", "arbitrary")),
        name="gla",
    )(proj, proj, proj, proj, a1, wa2, ba, gn, s0)


def _softplus_both(z):
    sp = jnp.maximum(z, 0.0) + jnp.log(1.0 + jnp.exp(-jnp.abs(z)))
    return sp, sp - z


def _sbp_kernel(q_ref, k_ref, v_ref, bias_ref, o_ref, *, bq, scale):
    i = pl.program_id(2)
    q = (q_ref[0] * scale).astype(bf16)
    hd = q.shape[-1]
    bias = bias_ref[0]
    sub = SB_SUB
    rj = lax.broadcasted_iota(jnp.int32, (sub, 2 * sub), 0)
    cs = lax.broadcasted_iota(jnp.int32, (sub, 2 * sub), 1)
    ntri = jnp.where((cs >= sub) | (rj > cs), -1.0, 0.0).astype(bf16)
    col_minus_row = (lax.broadcasted_iota(jnp.int32, (bq, SB_KEYS), 1)
                     - lax.broadcasted_iota(jnp.int32, (bq, SB_KEYS), 0))
    nt = (((1,), (1,)), ((), ()))
    steps_per_q = bq // SB_KEYS

    def step(kb, carry, acc, mask_offset):
        ks = pl.ds(pl.multiple_of(kb * SB_KEYS, SB_KEYS), SB_KEYS)
        k = k_ref[0, ks, :].astype(bf16)
        v = v_ref[0, ks, :].astype(bf16)
        z = lax.dot_general(q, k, nt, preferred_element_type=f32) + bias
        sp, spn = _softplus_both(z)
        if mask_offset is not None:
            past = col_minus_row < mask_offset
            sp = jnp.where(past, sp, 0.0)
        spb = sp.astype(bf16)
        r_hi = jnp.dot(spb[:, sub:], ntri, preferred_element_type=f32)
        r_lo = jnp.dot(spb[:, :sub], ntri, preferred_element_type=f32)
        c_mid = carry + r_hi[:, sub:]
        after = jnp.concatenate([c_mid + r_lo[:, :sub], carry + r_hi[:, :sub]], axis=1)
        a = jnp.exp(after - spn)
        if mask_offset is not None:
            a = jnp.where(past, a, 0.0)
        acc = acc + jnp.dot(a.astype(bf16), v, preferred_element_type=f32)
        return c_mid + r_lo[:, sub:], acc

    carry = jnp.zeros((bq, sub), f32)
    acc = jnp.zeros((bq, hd), f32)
    for d in range(steps_per_q - 1, -1, -1):
        carry, acc = step(i * steps_per_q + d, carry, acc, -d * SB_KEYS)

    def body(n, ca):
        kb = i * steps_per_q - 1 - 2 * n
        c, a = step(kb, ca[0], ca[1], None)
        return step(kb - 1, c, a, None)

    carry, acc = lax.fori_loop(0, i * (steps_per_q // 2), body, (carry, acc))
    o_ref[0] = acc.astype(o_ref.dtype)


def _sb_prompt(q, k, v, bias, heads, bq=512):
    bsz, t, d = q.shape
    hd = d // heads
    assert bq % (2 * SB_KEYS) == 0 and t % bq == 0
    kern = functools.partial(_sbp_kernel, bq=bq, scale=float(hd) ** -0.5)
    return pl.pallas_call(
        kern,
        grid=(bsz, heads, t // bq),
        in_specs=[
            pl.BlockSpec((1, bq, hd), lambda b, h, i: (b, i, h)),
            pl.BlockSpec((1, t, hd), lambda b, h, i: (b, 0, h)),
            pl.BlockSpec((1, t, hd), lambda b, h, i: (b, 0, h)),
            pl.BlockSpec((1, 1, SB_KEYS), lambda b, h, i: (h, 0, 0)),
        ],
        out_specs=pl.BlockSpec((1, bq, hd), lambda b, h, i: (b, i, h)),
        out_shape=jax.ShapeDtypeStruct((bsz, t, d), bf16),
        compiler_params=_cparams(("parallel", "parallel", "arbitrary")),
        name="sb_prompt",
    )(q, k, v, bias)


def _sbs_kernel(pt_ref, q_ref, kn_ref, vn_ref, *rest, heads, nq, scale, npp):
    k_refs, v_refs = rest[:npp], rest[npp:2 * npp]
    bias_ref, o_ref, qm_ref, acc_ref, carry_ref = rest[2 * npp:]
    p = pl.program_id(1)
    hd = k_refs[0].shape[2]
    ps = k_refs[0].shape[1] // heads
    d = heads * hd
    nc = heads * nq
    bias = bias_ref[...]
    rs = lax.broadcasted_iota(jnp.int32, (2 * ps, 2 * ps), 0)
    cj = lax.broadcasted_iota(jnp.int32, (2 * ps, 2 * ps), 1)
    tri = (cj > rs).astype(bf16)
    nt = (((1,), (1,)), ((), ()))
    tn = (((0,), (0,)), ((), ()))

    def gather(ref):
        return jnp.concatenate(
            [ref[0, pl.ds(h, ps, stride=heads), :].astype(bf16) for h in range(heads)], axis=1)

    def weights(kp, carry, tri_m, masked):
        n = kp.shape[0]
        z = lax.dot_general(kp, qm_ref[...], nt, preferred_element_type=f32) * scale + bias
        sp, spn = _softplus_both(z)
        if masked:
            key = lax.broadcasted_iota(jnp.int32, (n, nc), 0)
            qry = lax.broadcasted_iota(jnp.int32, (n, nc), 1) % nq
            past = key < qry
            sp = jnp.where(past, sp, 0.0)
        after = carry - jnp.dot(tri_m, sp.astype(bf16), preferred_element_type=f32)
        w = jnp.exp(after - spn)
        if masked:
            w = jnp.where(past, w, 0.0)
        return w.astype(bf16), carry - jnp.sum(sp, axis=0, keepdims=True)

    @pl.when(p == 0)
    def _():
        q = q_ref[0]
        qrep = jnp.concatenate([q] * heads, axis=0)
        rh = lax.broadcasted_iota(jnp.int32, (nc, d), 0) // nq
        ch = lax.broadcasted_iota(jnp.int32, (nc, d), 1) // hd
        qm_ref[...] = jnp.where(rh == ch, qrep, 0.0).astype(bf16)
        w, carry = weights(kn_ref[0].astype(bf16), jnp.zeros((1, nc), f32), tri[:ps, :ps], True)
        acc_ref[...] = lax.dot_general(w, vn_ref[0].astype(bf16), tn, preferred_element_type=f32)
        carry_ref[...] = jnp.broadcast_to(carry, carry_ref.shape)

    carry = carry_ref[0:1, :]
    ws = []
    for g in range(0, npp, 2):
        kp = jnp.concatenate([gather(k_refs[g + 1]), gather(k_refs[g])], axis=0)
        w, carry = weights(kp, carry, tri, False)
        ws.append(w)
    w_all = jnp.concatenate(ws[::-1], axis=0)
    v_all = jnp.concatenate([gather(r) for r in v_refs[::-1]], axis=0)
    acc_ref[...] += lax.dot_general(w_all, v_all, tn, preferred_element_type=f32)
    carry_ref[...] = jnp.broadcast_to(carry, carry_ref.shape)

    @pl.when(p == pl.num_programs(1) - 1)
    def _():
        for h in range(heads):
            o_ref[0, :, h * hd:(h + 1) * hd] = acc_ref[h * nq:(h + 1) * nq, h * hd:(h + 1) * hd].astype(o_ref.dtype)


def _sb_sample(q, k_new, v_new, cache_k, cache_v, page_table, bias, layer, heads, npp=4):
    bsz, nq, d = q.shape
    n_pages = page_table.shape[1]
    assert n_pages % npp == 0 and npp % 2 == 0
    hd = d // heads
    ps = cache_k.shape[2] // heads
    nc = heads * nq
    kern = functools.partial(_sbs_kernel, heads=heads, nq=nq, scale=float(hd) ** -0.5, npp=npp)
    cache_spec = lambda back: pl.BlockSpec(
        (None, 1, ps * heads, hd), lambda b, p, pt: (layer, pt[b, n_pages - back - npp * p], 0, 0))
    cache_specs = [cache_spec(r + 1) for r in range(npp)]
    grid_spec = pltpu.PrefetchScalarGridSpec(
        num_scalar_prefetch=1,
        grid=(bsz, n_pages // npp),
        in_specs=[
            pl.BlockSpec((1, nq, d), lambda b, p, pt: (b, 0, 0)),
            pl.BlockSpec((1, ps, d), lambda b, p, pt: (b, 0, 0)),
            pl.BlockSpec((1, ps, d), lambda b, p, pt: (b, 0, 0)),
            *cache_specs, *cache_specs,
            pl.BlockSpec((1, nc), lambda b, p, pt: (0, 0)),
        ],
        out_specs=pl.BlockSpec((1, nq, d), lambda b, p, pt: (b, 0, 0)),
        scratch_shapes=[
            pltpu.VMEM((nc, d), bf16),
            pltpu.VMEM((nc, d), f32),
            pltpu.VMEM((8, nc), f32),
        ],
    )
    return pl.pallas_call(
        kern,
        grid_spec=grid_spec,
        out_shape=jax.ShapeDtypeStruct((bsz, nq, d), bf16),
        compiler_params=_cparams(("parallel", "arbitrary")),
        name="sb_sample",
    )(page_table, q, k_new, v_new, *([cache_k] * npp), *([cache_v] * npp), bias)


def _pool_kernel(*refs, bm, has_halo, pos0, windows):
    if has_halo:
        x_ref, xh_ref, st_ref, g_ref, w_ref, sc_ref, o_ref, so_ref, ext_ref = refs
    else:
        x_ref, st_ref, g_ref, w_ref, sc_ref, o_ref, so_ref, ext_ref = refs
    i = pl.program_id(1)
    g = g_ref[...]
    x = x_ref[0]
    h = _rms(x, g)
    if has_halo:
        halo = jnp.where(i == 0, st_ref[0], _rms(xh_ref[0], g))
    else:
        halo = st_ref[0]
    ext_ref[0:POOL_HALO, :] = halo
    ext_ref[POOL_HALO:POOL_HALO + bm, :] = h
    gd = x.shape[-1] // len(windows)
    pos = pos0 + i * bm + lax.broadcasted_iota(jnp.int32, (bm, 1), 0)
    for gi, w in enumerate(windows):
        cs = slice(gi * gd, (gi + 1) * gd)
        hg = h[:, cs]
        win = hg
        for s in range(1, w):
            win = win + ext_ref[pl.ds(POOL_HALO - s, bm), cs]
        cnt = jnp.minimum(w, pos + 1).astype(f32)
        dlt = (win / cnt - hg).astype(bf16)
        y = jnp.dot(dlt, w_ref[gi], preferred_element_type=f32)
        o_ref[0, :, cs] = x[:, cs] + y * sc_ref[:, cs]
    so_ref[0] = ext_ref[pl.ds(bm, POOL_HALO), :]


def _pool(x, state, g, w, sc, layer, wlayer, bm, pos0):
    bsz, t, d = x.shape
    ng, gd = w.shape[1], w.shape[2]
    has_halo = t > bm
    hb = bm // POOL_HALO
    kern = functools.partial(_pool_kernel, bm=bm, has_halo=has_halo, pos0=pos0, windows=POOL_WINDOWS)
    in_specs = [pl.BlockSpec((1, bm, d), lambda b, i: (b, i, 0))]
    args = [x]
    if has_halo:
        in_specs.append(pl.BlockSpec((1, POOL_HALO, d), lambda b, i: (b, jnp.maximum(i * hb - 1, 0), 0)))
        args.append(x)
    in_specs += [
        pl.BlockSpec((1, POOL_HALO, d), lambda b, i: (b, 0, 0)),
        pl.BlockSpec((None, 1, d), lambda b, i: (layer, 0, 0)),
        pl.BlockSpec((None, ng, gd, gd), lambda b, i: (wlayer, 0, 0, 0)),
        pl.BlockSpec((None, 1, d), lambda b, i: (wlayer, 0, 0)),
    ]
    args += [state, g, w, sc]
    return pl.pallas_call(
        kern,
        grid=(bsz, t // bm),
        in_specs=in_specs,
        out_specs=[
            pl.BlockSpec((1, bm, d), lambda b, i: (b, i, 0)),
            pl.BlockSpec((1, POOL_HALO, d), lambda b, i: (b, 0, 0)),
        ],
        out_shape=[
            jax.ShapeDtypeStruct((bsz, t, d), f32),
            jax.ShapeDtypeStruct((bsz, POOL_HALO, d), f32),
        ],
        scratch_shapes=[pltpu.VMEM((POOL_HALO + bm, d), f32)],
        compiler_params=_cparams(("parallel", "arbitrary")),
        name="pool",
    )(*args)


def kernel(x_prompt, x_sample, state_gla, cache_sb_k, cache_sb_v, state_pool, page_table, p_prompt, p_sample, norm_ffn1, ffn1_gate, ffn1_up, ffn1_down, norm_mix, norm_ffn2, ffn2_gate, ffn2_up, ffn2_down, norm_ple, ple_gate, ple_proj, gla_in, gla_a1, gla_a2, gla_a_bias, gla_norm, gla_out, sb_in, sb_bias, sb_out, pool_w, pool_scale, norm_final):
    bp, t, d = x_prompt.shape
    bs, ts, _ = x_sample.shape
    depth = norm_ffn1.shape[0]
    mp, ms = bp * t, bs * ts
    kd = gla_a2.shape[-1]
    dk = kd // GLA_HEADS
    dv = gla_out.shape[1] // GLA_HEADS
    hd = d // SB_HEADS
    page = cache_sb_k.shape[2]
    pstate = state_pool.shape[2]

    row3 = lambda a: a.reshape(a.shape[0], 1, a.shape[-1])
    n_ffn1, n_mix, n_ffn2, n_ple = row3(norm_ffn1), row3(norm_mix), row3(norm_ffn2), row3(norm_ple)
    n_final = norm_final.reshape(1, d)
    w1g, w1u, w1d = ffn1_gate.astype(bf16), ffn1_up.astype(bf16), ffn1_down.astype(bf16)
    w2g, w2u, w2d = ffn2_gate.astype(bf16), ffn2_up.astype(bf16), ffn2_down.astype(bf16)
    wpg, wpp = ple_gate.astype(bf16), ple_proj.astype(bf16)
    n_gla = gla_in.shape[0]
    gla_w = gla_in.astype(bf16)
    gla_wa1 = jnp.pad(gla_a1, ((0, 0), (0, 0), (0, GLA_RANK_PAD - gla_a1.shape[-1]))).astype(bf16)
    gla_wa2 = jnp.concatenate(
        [gla_a2, jnp.zeros((n_gla, GLA_RANK_PAD - gla_a2.shape[1], kd), f32)], axis=1).astype(bf16)
    gla_ba = row3(gla_a_bias)
    gla_gn = gla_norm.reshape(n_gla, 1, GLA_HEADS * dv)
    gla_wo = gla_out.astype(bf16)
    sb_wi, sb_wo = sb_in.astype(bf16), sb_out.astype(bf16)
    pool_wb = pool_w.astype(bf16)
    pool_sc = row3(pool_scale)
    pp = p_prompt.reshape(depth, mp, -1)
    psm = p_sample.reshape(depth, ms, -1)

    xp = x_prompt.reshape(mp, d)
    xs = x_sample.reshape(ms, d)
    gla_p, gla_s, kp, vp, ksm, vsm, pool_p, pool_s = [], [], [], [], [], [], [], []
    for i in range(depth):
        kind, j = i % N_MIXERS, i // N_MIXERS
        xp = _ffn(xp, n_ffn1, w1g, w1u, w1d, i, bm=FFN_ROWS)
        xs = _ffn(xs, n_ffn1, w1g, w1u, w1d, i, bm=ms)
        if kind == 0:
            proj_p, a1_p = _norm_mm(xp, n_mix, gla_w, i, j, bm=1024, bn=512, w2=gla_wa1)
            proj_s, a1_s = _norm_mm(xs, n_mix, gla_w, i, j, bm=ms, bn=512, w2=gla_wa1)
            pad_s = lambda a: jnp.pad(a.reshape(bs, ts, -1), ((0, 0), (0, GLA_CHUNK - ts), (0, 0)))
            s0 = jnp.zeros((bp, GLA_HEADS, dk, dv), f32)
            og_p, st_p = _gla(proj_p.reshape(bp, t, -1), a1_p.reshape(bp, t, -1), gla_wa2, gla_ba, gla_gn,
                              s0, j, tb=256, valid=GLA_CHUNK)
            og_s, st_s = _gla(pad_s(proj_s), pad_s(a1_s), gla_wa2, gla_ba, gla_gn,
                              state_gla[j], j, tb=GLA_CHUNK, valid=ts)
            xp = _mm_res(og_p.reshape(mp, -1), gla_wo, xp, j, bm=1024, bn=1024)
            xs = _mm_res(og_s[:, :ts].reshape(ms, -1), gla_wo, xs, j, bm=ms, bn=1024)
            gla_p.append(st_p)
            gla_s.append(st_s)
        elif kind == 1:
            q_p, k_p, v_p = [_norm_mm(xp, n_mix, sb_wi, i, j, bm=1024, bn=512, col0=c * d, n=d).reshape(bp, t, d)
                             for c in range(3)]
            qkv_s = _norm_mm(xs, n_mix, sb_wi, i, j, bm=ms, bn=512)
            bias_p = jnp.broadcast_to(sb_bias[j][:, None, None], (SB_HEADS, 1, SB_KEYS))
            o_p = _sb_prompt(q_p, k_p, v_p, bias_p, SB_HEADS)
            k_s = qkv_s[:, d:2 * d].reshape(bs, ts, d)
            v_s = qkv_s[:, 2 * d:].reshape(bs, ts, d)
            pad = ((0, 0), (0, page - ts), (0, 0))
            bias_s = jnp.repeat(sb_bias[j], ts)[None, :]
            o_s = _sb_sample(qkv_s[:, :d].reshape(bs, ts, d), jnp.pad(k_s, pad), jnp.pad(v_s, pad),
                             cache_sb_k.reshape(cache_sb_k.shape[:2] + (page * SB_HEADS, hd)),
                             cache_sb_v.reshape(cache_sb_v.shape[:2] + (page * SB_HEADS, hd)),
                             page_table, bias_s, j, SB_HEADS)
            xp = _mm_res(o_p.reshape(mp, d), sb_wo, xp, j, bm=1024, bn=1024)
            xs = _mm_res(o_s.reshape(ms, d), sb_wo, xs, j, bm=ms, bn=1024)
            kp.append(k_p.reshape(bp, t, SB_HEADS, hd))
            vp.append(v_p.reshape(bp, t, SB_HEADS, hd))
            ksm.append(k_s.reshape(bs, ts, SB_HEADS, hd))
            vsm.append(v_s.reshape(bs, ts, SB_HEADS, hd))
        else:
            st0_p = jnp.zeros((bp, POOL_HALO, d), f32)
            st0_s = jnp.pad(state_pool[j], ((0, 0), (POOL_HALO - pstate, 0), (0, 0)))
            xp3, so_p = _pool(xp.reshape(bp, t, d), st0_p, n_mix, pool_wb, pool_sc, i, j, bm=512, pos0=0)
            xs3, so_s = _pool(xs.reshape(bs, ts, d), st0_s, n_mix, pool_wb, pool_sc, i, j, bm=ts,
                              pos0=page_table.shape[1] * page)
            xp, xs = xp3.reshape(mp, d), xs3.reshape(ms, d)
            pool_p.append(so_p[:, POOL_HALO - pstate:])
            pool_s.append(so_s[:, POOL_HALO - pstate:])
        xp = _ffn(xp, n_ffn2, w2g, w2u, w2d, i, bm=FFN_ROWS)
        xs = _ffn(xs, n_ffn2, w2g, w2u, w2d, i, bm=ms)
        final = i == depth - 1
        xp = _ple(xp, pp, n_ple, wpg, wpp, n_final, i, bm=512, final=final)
        xs = _ple(xs, psm, n_ple, wpg, wpp, n_final, i, bm=ms, final=final)
    return (xp.reshape(bp, t, d), xs.reshape(bs, ts, d), jnp.stack(gla_p), jnp.stack(gla_s),
            jnp.stack(kp), jnp.stack(vp), jnp.stack(ksm), jnp.stack(vsm),
            jnp.stack(pool_p), jnp.stack(pool_s))
```

```python
import functools

import jax
import jax.numpy as jnp
from jax import lax
from jax.experimental import pallas as pl
from jax.experimental.pallas import tpu as pltpu

f32 = jnp.float32
bf16 = jnp.bfloat16

EPS = 1e-6
N_MIXERS = 3
GLA_HEADS = 4
GLA_TAU = 16.0
GLA_CHUNK = 64
GLA_RANK_PAD = 128
SB_HEADS = 16
SB_KEYS = 256
SB_SUB = 128
POOL_WINDOWS = (2, 4, 8, 16)
POOL_HALO = 16
VMEM_LIMIT = 58 * 1024 * 1024
FFN_ROWS = 512


def _cparams(sem):
    return pltpu.CompilerParams(dimension_semantics=sem, vmem_limit_bytes=VMEM_LIMIT)


def _rms(x, g):
    return x * lax.rsqrt(jnp.mean(x * x, axis=-1, keepdims=True) + EPS) * g


def _ffn_kernel(x_ref, xs_ref, g_ref, wg_ref, wu_ref, wd_ref, o_ref, os_ref, h_ref, hs_ref):
    j = pl.program_id(1)

    def accumulate(x_ref, o_ref, h_ref):
        @pl.when(j == 0)
        def _():
            x = x_ref[...]
            h_ref[...] = _rms(x, g_ref[...]).astype(bf16)
            o_ref[...] = x

        h = h_ref[...]
        a = jnp.dot(h, wg_ref[...], preferred_element_type=f32)
        u = jnp.dot(h, wu_ref[...], preferred_element_type=f32)
        act = (0.5 * a * jax.nn.sigmoid(a) * u).astype(bf16)
        o_ref[...] += jnp.dot(act, wd_ref[...], preferred_element_type=f32)

    accumulate(x_ref, o_ref, h_ref)

    @pl.when(pl.program_id(0) == 0)
    def _():
        accumulate(xs_ref, os_ref, hs_ref)


def _ffn(x, xs, g, wg, wu, wd, layer, bm, bf=512):
    m, d = x.shape
    ms = xs.shape[0]
    f = wg.shape[-1]
    return pl.pallas_call(
        _ffn_kernel,
        grid=(m // bm, f // bf),
        in_specs=[
            pl.BlockSpec((bm, d), lambda i, j: (i, 0)),
            pl.BlockSpec((ms, d), lambda i, j: (0, 0)),
            pl.BlockSpec((None, 1, d), lambda i, j: (layer, 0, 0)),
            pl.BlockSpec((None, d, bf), lambda i, j: (layer, 0, j)),
            pl.BlockSpec((None, d, bf), lambda i, j: (layer, 0, j)),
            pl.BlockSpec((None, bf, d), lambda i, j: (layer, j, 0)),
        ],
        out_specs=[
            pl.BlockSpec((bm, d), lambda i, j: (i, 0)),
            pl.BlockSpec((ms, d), lambda i, j: (0, 0)),
        ],
        out_shape=[jax.ShapeDtypeStruct((m, d), f32), jax.ShapeDtypeStruct((ms, d), f32)],
        scratch_shapes=[pltpu.VMEM((bm, d), bf16), pltpu.VMEM((ms, d), bf16)],
        compiler_params=_cparams(("arbitrary", "arbitrary")),
        name="ffn",
    )(x, xs, g, wg, wu, wd)


def _ple_kernel(x_ref, p_ref, g_ref, wg_ref, wp_ref, gf_ref, o_ref, *, final):
    x = x_ref[...]
    h = _rms(x, g_ref[...]).astype(bf16)
    gate = jax.nn.sigmoid(jnp.dot(h, wg_ref[...], preferred_element_type=f32))
    proj = jnp.dot(p_ref[...].astype(bf16), wp_ref[...], preferred_element_type=f32)
    y = x + gate * proj
    if final:
        y = _rms(y, gf_ref[...])
    o_ref[...] = y


def _ple(x, p, g, wg, wp, gf, layer, bm, final):
    m, d = x.shape
    pd = p.shape[-1]
    return pl.pallas_call(
        functools.partial(_ple_kernel, final=final),
        grid=(m // bm,),
        in_specs=[
            pl.BlockSpec((bm, d), lambda i: (i, 0)),
            pl.BlockSpec((None, bm, pd), lambda i: (layer, i, 0)),
            pl.BlockSpec((None, 1, d), lambda i: (layer, 0, 0)),
            pl.BlockSpec((None, d, d), lambda i: (layer, 0, 0), pipeline_mode=pl.Buffered(1)),
            pl.BlockSpec((None, pd, d), lambda i: (layer, 0, 0), pipeline_mode=pl.Buffered(1)),
            pl.BlockSpec((1, d), lambda i: (0, 0)),
        ],
        out_specs=pl.BlockSpec((bm, d), lambda i: (i, 0)),
        out_shape=jax.ShapeDtypeStruct((m, d), f32),
        compiler_params=_cparams(("parallel",)),
        name="ple",
    )(x, p, g, wg, wp, gf)


def _norm_mm_kernel(*refs, has_aux):
    if has_aux:
        x_ref, g_ref, w_ref, w2_ref, o_ref, o2_ref, h_ref = refs
    else:
        x_ref, g_ref, w_ref, o_ref, h_ref = refs

    @pl.when(pl.program_id(1) == 0)
    def _():
        h_ref[...] = _rms(x_ref[...], g_ref[...]).astype(bf16)
        if has_aux:
            o2_ref[...] = jnp.dot(h_ref[...], w2_ref[...], preferred_element_type=f32)

    o_ref[...] = jnp.dot(h_ref[...], w_ref[...], preferred_element_type=f32)


def _norm_mm_split_kernel(x_ref, g_ref, w_ref, *rest, per):
    outs, h_ref = rest[:-1], rest[-1]
    j = pl.program_id(1)

    @pl.when(j == 0)
    def _():
        h_ref[...] = _rms(x_ref[...], g_ref[...]).astype(bf16)

    y = jnp.dot(h_ref[...], w_ref[...], preferred_element_type=f32)
    for s, o_ref in enumerate(outs):
        @pl.when((j >= s * per) & (j < (s + 1) * per))
        def _(o_ref=o_ref):
            o_ref[...] = y


def _norm_mm_split(x, g, w, layer, wlayer, bm, bn, nsplit):
    m, d = x.shape
    n = w.shape[-1]
    per = n // nsplit // bn
    assert n == nsplit * per * bn and m % bm == 0
    return pl.pallas_call(
        functools.partial(_norm_mm_split_kernel, per=per),
        grid=(m // bm, n // bn),
        in_specs=[
            pl.BlockSpec((bm, d), lambda i, j: (i, 0)),
            pl.BlockSpec((None, 1, d), lambda i, j: (layer, 0, 0)),
            pl.BlockSpec((None, d, bn), lambda i, j: (wlayer, 0, j)),
        ],
        out_specs=[pl.BlockSpec((bm, bn), lambda i, j, s=s: (i, jnp.clip(j - s * per, 0, per - 1)))
                   for s in range(nsplit)],
        out_shape=[jax.ShapeDtypeStruct((m, n // nsplit), f32)] * nsplit,
        scratch_shapes=[pltpu.VMEM((bm, d), bf16)],
        compiler_params=_cparams(("arbitrary", "arbitrary")),
        name="norm_mm_split",
    )(x, g, w)


def _norm_mm(x, g, w, layer, wlayer, bm, bn, w2=None):
    m, d = x.shape
    n = w.shape[-1]
    assert n % bn == 0 and m % bm == 0
    in_specs = [
        pl.BlockSpec((bm, d), lambda i, j: (i, 0)),
        pl.BlockSpec((None, 1, d), lambda i, j: (layer, 0, 0)),
        pl.BlockSpec((None, d, bn), lambda i, j: (wlayer, 0, j)),
    ]
    out_specs = [pl.BlockSpec((bm, bn), lambda i, j: (i, j))]
    out_shape = [jax.ShapeDtypeStruct((m, n), f32)]
    args = [x, g, w]
    if w2 is not None:
        n2 = w2.shape[-1]
        in_specs.append(pl.BlockSpec((None, d, n2), lambda i, j: (wlayer, 0, 0)))
        out_specs.append(pl.BlockSpec((bm, n2), lambda i, j: (i, 0)))
        out_shape.append(jax.ShapeDtypeStruct((m, n2), f32))
        args.append(w2)
    outs = pl.pallas_call(
        functools.partial(_norm_mm_kernel, has_aux=w2 is not None),
        grid=(m // bm, n // bn),
        in_specs=in_specs,
        out_specs=out_specs,
        out_shape=out_shape,
        scratch_shapes=[pltpu.VMEM((bm, d), bf16)],
        compiler_params=_cparams(("parallel", "arbitrary")),
        name="norm_mm",
    )(*args)
    return outs if w2 is not None else outs[0]


def _mm_res_kernel(a_ref, w_ref, x_ref, o_ref):
    o_ref[...] = x_ref[...] + jnp.dot(a_ref[...], w_ref[...], preferred_element_type=f32)


def _mm_res(a, w, x, wlayer, bm, bn):
    m, k = a.shape
    n = w.shape[-1]
    return pl.pallas_call(
        _mm_res_kernel,
        grid=(m // bm, n // bn),
        in_specs=[
            pl.BlockSpec((bm, k), lambda i, j: (i, 0)),
            pl.BlockSpec((None, k, bn), lambda i, j: (wlayer, 0, j)),
            pl.BlockSpec((bm, bn), lambda i, j: (i, j)),
        ],
        out_specs=pl.BlockSpec((bm, bn), lambda i, j: (i, j)),
        out_shape=jax.ShapeDtypeStruct((m, n), f32),
        compiler_params=_cparams(("parallel", "arbitrary")),
        name="mm_res",
    )(a, w, x)


def _split3(x):
    hi = x.astype(bf16)
    r1 = x - hi.astype(f32)
    mid = r1.astype(bf16)
    lo = (r1 - mid.astype(f32)).astype(bf16)
    return hi, mid, lo


def _gla_kernel(q_ref, k_ref, v_ref, r_ref, a1_ref, wa2_ref, ba_ref, gn_ref, s0_ref,
                og_ref, sout_ref, s_ref, *, chunk, n_chunks, valid, qscale):
    c_rows = chunk
    heads, dk, dv = s_ref.shape

    @pl.when(pl.program_id(1) == 0)
    def _():
        s_ref[...] = s0_ref[0]

    row = lax.broadcasted_iota(jnp.int32, (c_rows, c_rows), 0)
    col = lax.broadcasted_iota(jnp.int32, (c_rows, c_rows), 1)
    tril = col <= row
    tri_b = tril.astype(bf16)
    ones_b = jnp.ones((c_rows, 128), bf16)
    wa2 = wa2_ref[...]
    ba = ba_ref[...]
    gn = gn_ref[...]
    tn = (((0,), (0,)), ((), ()))
    nt = (((1,), (1,)), ((), ()))

    def body(c, carry):
        sl = pl.ds(pl.multiple_of(c * c_rows, c_rows), c_rows)
        x = jnp.dot(a1_ref[0, sl, :].astype(bf16), wa2, preferred_element_type=f32) + ba
        la = (jnp.minimum(x, 0.0) - jnp.log1p(jnp.exp(-jnp.abs(x)))) * (1.0 / GLA_TAU)
        if valid < c_rows:
            la = jnp.where(lax.broadcasted_iota(jnp.int32, la.shape, 0) < valid, la, 0.0)
        parts = _split3(la)
        b = sum(jnp.dot(tri_b, p, preferred_element_type=f32) for p in parts)
        b_last = b[c_rows - 1:c_rows, :]
        q = q_ref[0, sl, :] * qscale
        k = k_ref[0, sl, :]
        vb = v_ref[0, sl, :].astype(bf16)
        r = r_ref[0, sl, :]
        qd = (q * jnp.exp(b)).astype(bf16)
        ki = (k * jnp.exp(-b)).astype(bf16)
        ke = (k * jnp.exp(b_last - b)).astype(bf16)
        for h in range(heads):
            kc = slice(h * dk, (h + 1) * dk)
            vc = slice(h * dv, (h + 1) * dv)
            att = lax.dot_general(qd[:, kc], ki[:, kc], nt, preferred_element_type=f32)
            att = jnp.where(tril, att, 0.0).astype(bf16)
            s_old = s_ref[h]
            o = (jnp.dot(att, vb[:, vc], preferred_element_type=f32)
                 + jnp.dot(qd[:, kc], s_old.astype(bf16), preferred_element_type=f32))
            dl_col = sum(lax.dot_general(p[:, kc], ones_b, tn, preferred_element_type=f32) for p in parts)
            decay = jnp.concatenate([jnp.exp(dl_col)] * (dv // 128), axis=1)
            s_ref[h] = s_old * decay + lax.dot_general(ke[:, kc], vb[:, vc], tn, preferred_element_type=f32)
            o = o * lax.rsqrt(jnp.mean(o * o, axis=-1, keepdims=True) + EPS) * gn[:, vc]
            rh = r[:, vc]
            og_ref[0, sl, vc] = (o * (rh * jax.nn.sigmoid(rh))).astype(og_ref.dtype)
        return carry

    lax.fori_loop(0, n_chunks, body, 0)

    @pl.when(pl.program_id(1) == pl.num_programs(1) - 1)
    def _():
        sout_ref[0] = s_ref[...]


def _gla(proj, a1, wa2, ba, gn, s0, layer, tb, valid):
    bsz, t, _ = proj.shape
    heads = GLA_HEADS
    dk, dv = s0.shape[-2], s0.shape[-1]
    kd, vd = heads * dk, heads * dv
    assert vd == 2 * kd
    kern = functools.partial(_gla_kernel, chunk=GLA_CHUNK, n_chunks=tb // GLA_CHUNK, valid=valid,
                             qscale=float(dk) ** -0.5)
    return pl.pallas_call(
        kern,
        grid=(bsz, t // tb),
        in_specs=[
            pl.BlockSpec((1, tb, kd), lambda b, i: (b, i, 0)),
            pl.BlockSpec((1, tb, kd), lambda b, i: (b, i, 1)),
            pl.BlockSpec((1, tb, vd), lambda b, i: (b, i, 1)),
            pl.BlockSpec((1, tb, vd), lambda b, i: (b, i, 2)),
            pl.BlockSpec((1, tb, GLA_RANK_PAD), lambda b, i: (b, i, 0)),
            pl.BlockSpec((None, GLA_RANK_PAD, kd), lambda b, i: (layer, 0, 0)),
            pl.BlockSpec((None, 1, kd), lambda b, i: (layer, 0, 0)),
            pl.BlockSpec((None, 1, vd), lambda b, i: (layer, 0, 0)),
            pl.BlockSpec((1, heads, dk, dv), lambda b, i: (b, 0, 0, 0)),
        ],
        out_specs=[
            pl.BlockSpec((1, tb, vd), lambda b, i: (b, i, 0)),
            pl.BlockSpec((1, heads, dk, dv), lambda b, i: (b, 0, 0, 0)),
        ],
        out_shape=[
            jax.ShapeDtypeStruct((bsz, t, vd), bf16),
            jax.ShapeDtypeStruct((bsz, heads, dk, dv), f32),
        ],
        scratch_shapes=[pltpu.VMEM((heads, dk, dv), f32)],
        compiler_params=_cparams(("parallel", "arbitrary")),
        name="gla",
    )(proj, proj, proj, proj, a1, wa2, ba, gn, s0)


def _softplus_both(z):
    sp = jnp.maximum(z, 0.0) + jnp.log(1.0 + jnp.exp(-jnp.abs(z)))
    return sp, sp - z


def _sbp_kernel(q_ref, k_ref, v_ref, bias_ref, o_ref, *, bq, scale):
    i = pl.program_id(2)
    q = (q_ref[0] * scale).astype(bf16)
    hd = q.shape[-1]
    bias = bias_ref[0]
    sub = SB_SUB
    rj = lax.broadcasted_iota(jnp.int32, (sub, 2 * sub), 0)
    cs = lax.broadcasted_iota(jnp.int32, (sub, 2 * sub), 1)
    ntri = jnp.where((cs >= sub) | (rj > cs), -1.0, 0.0).astype(bf16)
    col_minus_row = (lax.broadcasted_iota(jnp.int32, (bq, SB_KEYS), 1)
                     - lax.broadcasted_iota(jnp.int32, (bq, SB_KEYS), 0))
    nt = (((1,), (1,)), ((), ()))
    steps_per_q = bq // SB_KEYS

    def step(kb, carry, acc, mask_offset):
        ks = pl.ds(pl.multiple_of(kb * SB_KEYS, SB_KEYS), SB_KEYS)
        k = k_ref[0, ks, :].astype(bf16)
        v = v_ref[0, ks, :].astype(bf16)
        z = lax.dot_general(q, k, nt, preferred_element_type=f32) + bias
        sp, spn = _softplus_both(z)
        if mask_offset is not None:
            past = col_minus_row < mask_offset
            sp = jnp.where(past, sp, 0.0)
        spb = sp.astype(bf16)
        r_hi = jnp.dot(spb[:, sub:], ntri, preferred_element_type=f32)
        r_lo = jnp.dot(spb[:, :sub], ntri, preferred_element_type=f32)
        c_mid = carry + r_hi[:, sub:]
        after = jnp.concatenate([c_mid + r_lo[:, :sub], carry + r_hi[:, :sub]], axis=1)
        a = jnp.exp(after - spn)
        if mask_offset is not None:
            a = jnp.where(past, a, 0.0)
        acc = acc + jnp.dot(a.astype(bf16), v, preferred_element_type=f32)
        return c_mid + r_lo[:, sub:], acc

    carry = jnp.zeros((bq, sub), f32)
    acc = jnp.zeros((bq, hd), f32)
    for d in range(steps_per_q - 1, -1, -1):
        carry, acc = step(i * steps_per_q + d, carry, acc, -d * SB_KEYS)

    def body(n, ca):
        kb = i * steps_per_q - 1 - 2 * n
        c, a = step(kb, ca[0], ca[1], None)
        return step(kb - 1, c, a, None)

    carry, acc = lax.fori_loop(0, i * (steps_per_q // 2), body, (carry, acc))
    o_ref[0] = acc.astype(o_ref.dtype)


def _sb_prompt(q, k, v, bias, heads, bq=512):
    bsz, t, d = q.shape
    hd = d // heads
    assert bq % (2 * SB_KEYS) == 0 and t % bq == 0
    kern = functools.partial(_sbp_kernel, bq=bq, scale=float(hd) ** -0.5)
    return pl.pallas_call(
        kern,
        grid=(bsz, heads, t // bq),
        in_specs=[
            pl.BlockSpec((1, bq, hd), lambda b, h, i: (b, i, h)),
            pl.BlockSpec((1, t, hd), lambda b, h, i: (b, 0, h)),
            pl.BlockSpec((1, t, hd), lambda b, h, i: (b, 0, h)),
            pl.BlockSpec((1, 1, SB_KEYS), lambda b, h, i: (h, 0, 0)),
        ],
        out_specs=pl.BlockSpec((1, bq, hd), lambda b, h, i: (b, i, h)),
        out_shape=jax.ShapeDtypeStruct((bsz, t, d), bf16),
        compiler_params=_cparams(("parallel", "parallel", "arbitrary")),
        name="sb_prompt",
    )(q, k, v, bias)


def _sbs_kernel(pt_ref, q_ref, kn_ref, vn_ref, *rest, heads, nq, scale, npp):
    k_refs, v_refs = rest[:npp], rest[npp:2 * npp]
    bias_ref, o_ref, qm_ref, acc_ref, carry_ref = rest[2 * npp:]
    p = pl.program_id(1)
    hd = k_refs[0].shape[2]
    ps = k_refs[0].shape[1] // heads
    d = heads * hd
    nc = heads * nq
    bias = bias_ref[...]
    rs = lax.broadcasted_iota(jnp.int32, (2 * ps, 2 * ps), 0)
    cj = lax.broadcasted_iota(jnp.int32, (2 * ps, 2 * ps), 1)
    tri = (cj > rs).astype(bf16)
    nt = (((1,), (1,)), ((), ()))
    tn = (((0,), (0,)), ((), ()))

    def gather(ref):
        return jnp.concatenate(
            [ref[0, pl.ds(h, ps, stride=heads), :].astype(bf16) for h in range(heads)], axis=1)

    def weights(kp, carry, tri_m, masked):
        n = kp.shape[0]
        z = lax.dot_general(kp, qm_ref[...], nt, preferred_element_type=f32) * scale + bias
        sp, spn = _softplus_both(z)
        if masked:
            key = lax.broadcasted_iota(jnp.int32, (n, nc), 0)
            qry = lax.broadcasted_iota(jnp.int32, (n, nc), 1) % nq
            past = key < qry
            sp = jnp.where(past, sp, 0.0)
        after = carry - jnp.dot(tri_m, sp.astype(bf16), preferred_element_type=f32)
        w = jnp.exp(after - spn)
        if masked:
            w = jnp.where(past, w, 0.0)
        return w.astype(bf16), carry - jnp.sum(sp, axis=0, keepdims=True)

    @pl.when(p == 0)
    def _():
        q = q_ref[0]
        qrep = jnp.concatenate([q] * heads, axis=0)
        rh = lax.broadcasted_iota(jnp.int32, (nc, d), 0) // nq
        ch = lax.broadcasted_iota(jnp.int32, (nc, d), 1) // hd
        qm_ref[...] = jnp.where(rh == ch, qrep, 0.0).astype(bf16)
        w, carry = weights(kn_ref[0].astype(bf16), jnp.zeros((1, nc), f32), tri[:ps, :ps], True)
        acc_ref[...] = lax.dot_general(w, vn_ref[0].astype(bf16), tn, preferred_element_type=f32)
        carry_ref[...] = jnp.broadcast_to(carry, carry_ref.shape)

    carry = carry_ref[0:1, :]
    ws = []
    for g in range(0, npp, 2):
        kp = jnp.concatenate([gather(k_refs[g + 1]), gather(k_refs[g])], axis=0)
        w, carry = weights(kp, carry, tri, False)
        ws.append(w)
    w_all = jnp.concatenate(ws[::-1], axis=0)
    v_all = jnp.concatenate([gather(r) for r in v_refs[::-1]], axis=0)
    acc_ref[...] += lax.dot_general(w_all, v_all, tn, preferred_element_type=f32)
    carry_ref[...] = jnp.broadcast_to(carry, carry_ref.shape)

    @pl.when(p == pl.num_programs(1) - 1)
    def _():
        for h in range(heads):
            o_ref[0, :, h * hd:(h + 1) * hd] = acc_ref[h * nq:(h + 1) * nq, h * hd:(h + 1) * hd].astype(o_ref.dtype)


def _sb_sample(q, k_new, v_new, cache_k, cache_v, page_table, bias, layer, heads, npp=4):
    bsz, nq, d = q.shape
    n_pages = page_table.shape[1]
    assert n_pages % npp == 0 and npp % 2 == 0
    hd = d // heads
    ps = cache_k.shape[2] // heads
    nc = heads * nq
    kern = functools.partial(_sbs_kernel, heads=heads, nq=nq, scale=float(hd) ** -0.5, npp=npp)
    cache_spec = lambda back: pl.BlockSpec(
        (None, 1, ps * heads, hd), lambda b, p, pt: (layer, pt[b, n_pages - back - npp * p], 0, 0))
    cache_specs = [cache_spec(r + 1) for r in range(npp)]
    grid_spec = pltpu.PrefetchScalarGridSpec(
        num_scalar_prefetch=1,
        grid=(bsz, n_pages // npp),
        in_specs=[
            pl.BlockSpec((1, nq, d), lambda b, p, pt: (b, 0, 0)),
            pl.BlockSpec((1, ps, d), lambda b, p, pt: (b, 0, 0)),
            pl.BlockSpec((1, ps, d), lambda b, p, pt: (b, 0, 0)),
            *cache_specs, *cache_specs,
            pl.BlockSpec((1, nc), lambda b, p, pt: (0, 0)),
        ],
        out_specs=pl.BlockSpec((1, nq, d), lambda b, p, pt: (b, 0, 0)),
        scratch_shapes=[
            pltpu.VMEM((nc, d), bf16),
            pltpu.VMEM((nc, d), f32),
            pltpu.VMEM((8, nc), f32),
        ],
    )
    return pl.pallas_call(
        kern,
        grid_spec=grid_spec,
        out_shape=jax.ShapeDtypeStruct((bsz, nq, d), bf16),
        compiler_params=_cparams(("parallel", "arbitrary")),
        name="sb_sample",
    )(page_table, q, k_new, v_new, *([cache_k] * npp), *([cache_v] * npp), bias)


def _pool_kernel(*refs, bm, has_halo, pos0, windows):
    if has_halo:
        x_ref, xh_ref, st_ref, g_ref, w_ref, sc_ref, o_ref, so_ref, ext_ref = refs
    else:
        x_ref, st_ref, g_ref, w_ref, sc_ref, o_ref, so_ref, ext_ref = refs
    i = pl.program_id(1)
    g = g_ref[...]
    x = x_ref[0]
    h = _rms(x, g)
    if has_halo:
        halo = jnp.where(i == 0, st_ref[0], _rms(xh_ref[0], g))
    else:
        halo = st_ref[0]
    ext_ref[0:POOL_HALO, :] = halo
    ext_ref[POOL_HALO:POOL_HALO + bm, :] = h
    gd = x.shape[-1] // len(windows)
    pos = pos0 + i * bm + lax.broadcasted_iota(jnp.int32, (bm, 1), 0)
    for gi, w in enumerate(windows):
        cs = slice(gi * gd, (gi + 1) * gd)
        hg = h[:, cs]
        win = hg
        for s in range(1, w):
            win = win + ext_ref[pl.ds(POOL_HALO - s, bm), cs]
        cnt = jnp.minimum(w, pos + 1).astype(f32)
        dlt = (win / cnt - hg).astype(bf16)
        y = jnp.dot(dlt, w_ref[gi], preferred_element_type=f32)
        o_ref[0, :, cs] = x[:, cs] + y * sc_ref[:, cs]
    so_ref[0] = ext_ref[pl.ds(bm, POOL_HALO), :]


def _pool(x, state, g, w, sc, layer, wlayer, bm, pos0):
    bsz, t, d = x.shape
    ng, gd = w.shape[1], w.shape[2]
    has_halo = t > bm
    hb = bm // POOL_HALO
    kern = functools.partial(_pool_kernel, bm=bm, has_halo=has_halo, pos0=pos0, windows=POOL_WINDOWS)
    in_specs = [pl.BlockSpec((1, bm, d), lambda b, i: (b, i, 0))]
    args = [x]
    if has_halo:
        in_specs.append(pl.BlockSpec((1, POOL_HALO, d), lambda b, i: (b, jnp.maximum(i * hb - 1, 0), 0)))
        args.append(x)
    in_specs += [
        pl.BlockSpec((1, POOL_HALO, d), lambda b, i: (b, 0, 0)),
        pl.BlockSpec((None, 1, d), lambda b, i: (layer, 0, 0)),
        pl.BlockSpec((None, ng, gd, gd), lambda b, i: (wlayer, 0, 0, 0)),
        pl.BlockSpec((None, 1, d), lambda b, i: (wlayer, 0, 0)),
    ]
    args += [state, g, w, sc]
    return pl.pallas_call(
        kern,
        grid=(bsz, t // bm),
        in_specs=in_specs,
        out_specs=[
            pl.BlockSpec((1, bm, d), lambda b, i: (b, i, 0)),
            pl.BlockSpec((1, POOL_HALO, d), lambda b, i: (b, 0, 0)),
        ],
        out_shape=[
            jax.ShapeDtypeStruct((bsz, t, d), f32),
            jax.ShapeDtypeStruct((bsz, POOL_HALO, d), f32),
        ],
        scratch_shapes=[pltpu.VMEM((POOL_HALO + bm, d), f32)],
        compiler_params=_cparams(("parallel", "arbitrary")),
        name="pool",
    )(*args)


def kernel(x_prompt, x_sample, state_gla, cache_sb_k, cache_sb_v, state_pool, page_table, p_prompt, p_sample, norm_ffn1, ffn1_gate, ffn1_up, ffn1_down, norm_mix, norm_ffn2, ffn2_gate, ffn2_up, ffn2_down, norm_ple, ple_gate, ple_proj, gla_in, gla_a1, gla_a2, gla_a_bias, gla_norm, gla_out, sb_in, sb_bias, sb_out, pool_w, pool_scale, norm_final):
    bp, t, d = x_prompt.shape
    bs, ts, _ = x_sample.shape
    depth = norm_ffn1.shape[0]
    mp, ms = bp * t, bs * ts
    kd = gla_a2.shape[-1]
    dk = kd // GLA_HEADS
    dv = gla_out.shape[1] // GLA_HEADS
    hd = d // SB_HEADS
    page = cache_sb_k.shape[2]
    pstate = state_pool.shape[2]

    row3 = lambda a: a.reshape(a.shape[0], 1, a.shape[-1])
    n_ffn1, n_mix, n_ffn2, n_ple = row3(norm_ffn1), row3(norm_mix), row3(norm_ffn2), row3(norm_ple)
    n_final = norm_final.reshape(1, d)
    w1g, w1u, w1d = ffn1_gate.astype(bf16), ffn1_up.astype(bf16), ffn1_down.astype(bf16)
    w2g, w2u, w2d = ffn2_gate.astype(bf16), ffn2_up.astype(bf16), ffn2_down.astype(bf16)
    wpg, wpp = ple_gate.astype(bf16), ple_proj.astype(bf16)
    n_gla = gla_in.shape[0]
    gla_w = gla_in.astype(bf16)
    gla_wa1 = jnp.pad(gla_a1, ((0, 0), (0, 0), (0, GLA_RANK_PAD - gla_a1.shape[-1]))).astype(bf16)
    gla_wa2 = jnp.concatenate(
        [gla_a2, jnp.zeros((n_gla, GLA_RANK_PAD - gla_a2.shape[1], kd), f32)], axis=1).astype(bf16)
    gla_ba = row3(gla_a_bias)
    gla_gn = gla_norm.reshape(n_gla, 1, GLA_HEADS * dv)
    gla_wo = gla_out.astype(bf16)
    sb_wi, sb_wo = sb_in.astype(bf16), sb_out.astype(bf16)
    pool_wb = pool_w.astype(bf16)
    pool_sc = row3(pool_scale)
    pp = p_prompt.reshape(depth, mp, -1)
    psm = p_sample.reshape(depth, ms, -1)

    xp = x_prompt.reshape(mp, d)
    xs = x_sample.reshape(ms, d)
    gla_p, gla_s, kp, vp, ksm, vsm, pool_p, pool_s = [], [], [], [], [], [], [], []
    for i in range(depth):
        kind, j = i % N_MIXERS, i // N_MIXERS
        xp, xs = _ffn(xp, xs, n_ffn1, w1g, w1u, w1d, i, bm=FFN_ROWS)
        if kind == 0:
            proj_p, a1_p = _norm_mm(xp, n_mix, gla_w, i, j, bm=1024, bn=512, w2=gla_wa1)
            proj_s, a1_s = _norm_mm(xs, n_mix, gla_w, i, j, bm=ms, bn=512, w2=gla_wa1)
            pad_s = lambda a: jnp.pad(a.reshape(bs, ts, -1), ((0, 0), (0, GLA_CHUNK - ts), (0, 0)))
            s0 = jnp.zeros((bp, GLA_HEADS, dk, dv), f32)
            og_p, st_p = _gla(proj_p.reshape(bp, t, -1), a1_p.reshape(bp, t, -1), gla_wa2, gla_ba, gla_gn,
                              s0, j, tb=256, valid=GLA_CHUNK)
            og_s, st_s = _gla(pad_s(proj_s), pad_s(a1_s), gla_wa2, gla_ba, gla_gn,
                              state_gla[j], j, tb=GLA_CHUNK, valid=ts)
            xp = _mm_res(og_p.reshape(mp, -1), gla_wo, xp, j, bm=1024, bn=1024)
            xs = _mm_res(og_s[:, :ts].reshape(ms, -1), gla_wo, xs, j, bm=ms, bn=1024)
            gla_p.append(st_p)
            gla_s.append(st_s)
        elif kind == 1:
            q_p, k_p, v_p = [a.reshape(bp, t, d)
                             for a in _norm_mm_split(xp, n_mix, sb_wi, i, j, bm=1024, bn=512, nsplit=3)]
            qkv_s = _norm_mm(xs, n_mix, sb_wi, i, j, bm=ms, bn=512)
            bias_p = jnp.broadcast_to(sb_bias[j][:, None, None], (SB_HEADS, 1, SB_KEYS))
            o_p = _sb_prompt(q_p, k_p, v_p, bias_p, SB_HEADS)
            k_s = qkv_s[:, d:2 * d].reshape(bs, ts, d)
            v_s = qkv_s[:, 2 * d:].reshape(bs, ts, d)
            pad = ((0, 0), (0, page - ts), (0, 0))
            bias_s = jnp.repeat(sb_bias[j], ts)[None, :]
            o_s = _sb_sample(qkv_s[:, :d].reshape(bs, ts, d), jnp.pad(k_s, pad), jnp.pad(v_s, pad),
                             cache_sb_k.reshape(cache_sb_k.shape[:2] + (page * SB_HEADS, hd)),
                             cache_sb_v.reshape(cache_sb_v.shape[:2] + (page * SB_HEADS, hd)),
                             page_table, bias_s, j, SB_HEADS)
            xp = _mm_res(o_p.reshape(mp, d), sb_wo, xp, j, bm=1024, bn=1024)
            xs = _mm_res(o_s.reshape(ms, d), sb_wo, xs, j, bm=ms, bn=1024)
            kp.append(k_p.reshape(bp, t, SB_HEADS, hd))
            vp.append(v_p.reshape(bp, t, SB_HEADS, hd))
            ksm.append(k_s.reshape(bs, ts, SB_HEADS, hd))
            vsm.append(v_s.reshape(bs, ts, SB_HEADS, hd))
        else:
            st0_p = jnp.zeros((bp, POOL_HALO, d), f32)
            st0_s = jnp.pad(state_pool[j], ((0, 0), (POOL_HALO - pstate, 0), (0, 0)))
            xp3, so_p = _pool(xp.reshape(bp, t, d), st0_p, n_mix, pool_wb, pool_sc, i, j, bm=512, pos0=0)
            xs3, so_s = _pool(xs.reshape(bs, ts, d), st0_s, n_mix, pool_wb, pool_sc, i, j, bm=ts,
                              pos0=page_table.shape[1] * page)
            xp, xs = xp3.reshape(mp, d), xs3.reshape(ms, d)
            pool_p.append(so_p[:, POOL_HALO - pstate:])
            pool_s.append(so_s[:, POOL_HALO - pstate:])
        xp, xs = _ffn(xp, xs, n_ffn2, w2g, w2u, w2d, i, bm=FFN_ROWS)
        final = i == depth - 1
        xp = _ple(xp, pp, n_ple, wpg, wpp, n_final, i, bm=512, final=final)
        xs = _ple(xs, psm, n_ple, wpg, wpp, n_final, i, bm=ms, final=final)
    return (xp.reshape(bp, t, d), xs.reshape(bs, ts, d), jnp.stack(gla_p), jnp.stack(gla_s),
            jnp.stack(kp), jnp.stack(vp), jnp.stack(ksm), jnp.stack(vsm),
            jnp.stack(pool_p), jnp.stack(pool_s))
```

```python
import functools

import jax
import jax.numpy as jnp
from jax import lax
from jax.experimental import pallas as pl
from jax.experimental.pallas import tpu as pltpu

f32 = jnp.float32
bf16 = jnp.bfloat16

EPS = 1e-6
N_MIXERS = 3
GLA_HEADS = 4
GLA_TAU = 16.0
GLA_CHUNK = 64
GLA_RANK_PAD = 128
SB_HEADS = 16
SB_KEYS = 256
SB_SUB = 128
POOL_WINDOWS = (2, 4, 8, 16)
POOL_HALO = 16
VMEM_LIMIT = 58 * 1024 * 1024
FFN_ROWS = 512


def _cparams(sem):
    return pltpu.CompilerParams(dimension_semantics=sem, vmem_limit_bytes=VMEM_LIMIT)


def _rms(x, g):
    return x * lax.rsqrt(jnp.mean(x * x, axis=-1, keepdims=True) + EPS) * g


def _ffn_kernel(x_ref, xs_ref, g_ref, wg_ref, wu_ref, wd_ref, o_ref, os_ref, h_ref):
    i = pl.program_id(0)
    j = pl.program_id(1)
    bm = x_ref.shape[0]
    ms = xs_ref.shape[0]

    def swiglu_down(h):
        a = jnp.dot(h, wg_ref[...], preferred_element_type=f32)
        u = jnp.dot(h, wu_ref[...], preferred_element_type=f32)
        act = (0.5 * a * jax.nn.sigmoid(a) * u).astype(bf16)
        return jnp.dot(act, wd_ref[...], preferred_element_type=f32)

    @pl.when(j == 0)
    def _():
        x = x_ref[...]
        h_ref[0:bm, :] = _rms(x, g_ref[...]).astype(bf16)
        o_ref[...] = x

    @pl.when((j == 0) & (i == 0))
    def _():
        xs = xs_ref[...]
        h_ref[bm:bm + ms, :] = _rms(xs, g_ref[...]).astype(bf16)
        os_ref[...] = xs

    @pl.when(i == 0)
    def _():
        y = swiglu_down(h_ref[...])
        o_ref[...] += y[:bm]
        os_ref[...] += y[bm:]

    @pl.when(i != 0)
    def _():
        o_ref[...] += swiglu_down(h_ref[0:bm, :])


def _ffn(x, xs, g, wg, wu, wd, layer, bm, bf=512):
    m, d = x.shape
    ms = xs.shape[0]
    f = wg.shape[-1]
    return pl.pallas_call(
        _ffn_kernel,
        grid=(m // bm, f // bf),
        in_specs=[
            pl.BlockSpec((bm, d), lambda i, j: (i, 0)),
            pl.BlockSpec((ms, d), lambda i, j: (0, 0)),
            pl.BlockSpec((None, 1, d), lambda i, j: (layer, 0, 0)),
            pl.BlockSpec((None, d, bf), lambda i, j: (layer, 0, j)),
            pl.BlockSpec((None, d, bf), lambda i, j: (layer, 0, j)),
            pl.BlockSpec((None, bf, d), lambda i, j: (layer, j, 0)),
        ],
        out_specs=[
            pl.BlockSpec((bm, d), lambda i, j: (i, 0)),
            pl.BlockSpec((ms, d), lambda i, j: (0, 0)),
        ],
        out_shape=[jax.ShapeDtypeStruct((m, d), f32), jax.ShapeDtypeStruct((ms, d), f32)],
        scratch_shapes=[pltpu.VMEM((bm + ms, d), bf16)],
        compiler_params=_cparams(("arbitrary", "arbitrary")),
        name="ffn",
    )(x, xs, g, wg, wu, wd)


def _ple_kernel(x_ref, p_ref, g_ref, wg_ref, wp_ref, gf_ref, o_ref, *, final):
    x = x_ref[...]
    h = _rms(x, g_ref[...]).astype(bf16)
    gate = jax.nn.sigmoid(jnp.dot(h, wg_ref[...], preferred_element_type=f32))
    proj = jnp.dot(p_ref[...].astype(bf16), wp_ref[...], preferred_element_type=f32)
    y = x + gate * proj
    if final:
        y = _rms(y, gf_ref[...])
    o_ref[...] = y


def _ple(x, p, g, wg, wp, gf, layer, bm, final):
    m, d = x.shape
    pd = p.shape[-1]
    return pl.pallas_call(
        functools.partial(_ple_kernel, final=final),
        grid=(m // bm,),
        in_specs=[
            pl.BlockSpec((bm, d), lambda i: (i, 0)),
            pl.BlockSpec((None, bm, pd), lambda i: (layer, i, 0)),
            pl.BlockSpec((None, 1, d), lambda i: (layer, 0, 0)),
            pl.BlockSpec((None, d, d), lambda i: (layer, 0, 0), pipeline_mode=pl.Buffered(1)),
            pl.BlockSpec((None, pd, d), lambda i: (layer, 0, 0), pipeline_mode=pl.Buffered(1)),
            pl.BlockSpec((1, d), lambda i: (0, 0)),
        ],
        out_specs=pl.BlockSpec((bm, d), lambda i: (i, 0)),
        out_shape=jax.ShapeDtypeStruct((m, d), f32),
        compiler_params=_cparams(("parallel",)),
        name="ple",
    )(x, p, g, wg, wp, gf)


def _norm_mm_kernel(*refs, has_aux):
    if has_aux:
        x_ref, g_ref, w_ref, w2_ref, o_ref, o2_ref, h_ref = refs
    else:
        x_ref, g_ref, w_ref, o_ref, h_ref = refs

    @pl.when(pl.program_id(1) == 0)
    def _():
        h_ref[...] = _rms(x_ref[...], g_ref[...]).astype(bf16)
        if has_aux:
            o2_ref[...] = jnp.dot(h_ref[...], w2_ref[...], preferred_element_type=f32)

    o_ref[...] = jnp.dot(h_ref[...], w_ref[...], preferred_element_type=f32)


def _norm_mm_split_kernel(x_ref, g_ref, w_ref, *rest, per):
    outs, h_ref = rest[:-1], rest[-1]
    j = pl.program_id(1)

    @pl.when(j == 0)
    def _():
        h_ref[...] = _rms(x_ref[...], g_ref[...]).astype(bf16)

    y = jnp.dot(h_ref[...], w_ref[...], preferred_element_type=f32)
    for s, o_ref in enumerate(outs):
        @pl.when((j >= s * per) & (j < (s + 1) * per))
        def _(o_ref=o_ref):
            o_ref[...] = y


def _norm_mm_split(x, g, w, layer, wlayer, bm, bn, nsplit):
    m, d = x.shape
    n = w.shape[-1]
    per = n // nsplit // bn
    assert n == nsplit * per * bn and m % bm == 0
    return pl.pallas_call(
        functools.partial(_norm_mm_split_kernel, per=per),
        grid=(m // bm, n // bn),
        in_specs=[
            pl.BlockSpec((bm, d), lambda i, j: (i, 0)),
            pl.BlockSpec((None, 1, d), lambda i, j: (layer, 0, 0)),
            pl.BlockSpec((None, d, bn), lambda i, j: (wlayer, 0, j)),
        ],
        out_specs=[pl.BlockSpec((bm, bn), lambda i, j, s=s: (i, jnp.clip(j - s * per, 0, per - 1)))
                   for s in range(nsplit)],
        out_shape=[jax.ShapeDtypeStruct((m, n // nsplit), f32)] * nsplit,
        scratch_shapes=[pltpu.VMEM((bm, d), bf16)],
        compiler_params=_cparams(("arbitrary", "arbitrary")),
        name="norm_mm_split",
    )(x, g, w)


def _norm_mm(x, g, w, layer, wlayer, bm, bn, w2=None):
    m, d = x.shape
    n = w.shape[-1]
    assert n % bn == 0 and m % bm == 0
    in_specs = [
        pl.BlockSpec((bm, d), lambda i, j: (i, 0)),
        pl.BlockSpec((None, 1, d), lambda i, j: (layer, 0, 0)),
        pl.BlockSpec((None, d, bn), lambda i, j: (wlayer, 0, j)),
    ]
    out_specs = [pl.BlockSpec((bm, bn), lambda i, j: (i, j))]
    out_shape = [jax.ShapeDtypeStruct((m, n), f32)]
    args = [x, g, w]
    if w2 is not None:
        n2 = w2.shape[-1]
        in_specs.append(pl.BlockSpec((None, d, n2), lambda i, j: (wlayer, 0, 0)))
        out_specs.append(pl.BlockSpec((bm, n2), lambda i, j: (i, 0)))
        out_shape.append(jax.ShapeDtypeStruct((m, n2), f32))
        args.append(w2)
    outs = pl.pallas_call(
        functools.partial(_norm_mm_kernel, has_aux=w2 is not None),
        grid=(m // bm, n // bn),
        in_specs=in_specs,
        out_specs=out_specs,
        out_shape=out_shape,
        scratch_shapes=[pltpu.VMEM((bm, d), bf16)],
        compiler_params=_cparams(("parallel", "arbitrary")),
        name="norm_mm",
    )(*args)
    return outs if w2 is not None else outs[0]


def _mm_res_kernel(a_ref, w_ref, x_ref, o_ref):
    o_ref[...] = x_ref[...] + jnp.dot(a_ref[...], w_ref[...], preferred_element_type=f32)


def _mm_res(a, w, x, wlayer, bm, bn):
    m, k = a.shape
    n = w.shape[-1]
    return pl.pallas_call(
        _mm_res_kernel,
        grid=(m // bm, n // bn),
        in_specs=[
            pl.BlockSpec((bm, k), lambda i, j: (i, 0)),
            pl.BlockSpec((None, k, bn), lambda i, j: (wlayer, 0, j)),
            pl.BlockSpec((bm, bn), lambda i, j: (i, j)),
        ],
        out_specs=pl.BlockSpec((bm, bn), lambda i, j: (i, j)),
        out_shape=jax.ShapeDtypeStruct((m, n), f32),
        compiler_params=_cparams(("parallel", "arbitrary")),
        name="mm_res",
    )(a, w, x)


def _split3(x):
    hi = x.astype(bf16)
    r1 = x - hi.astype(f32)
    mid = r1.astype(bf16)
    lo = (r1 - mid.astype(f32)).astype(bf16)
    return hi, mid, lo


def _gla_kernel(q_ref, k_ref, v_ref, r_ref, a1_ref, wa2_ref, ba_ref, gn_ref, s0_ref,
                og_ref, sout_ref, s_ref, *, chunk, n_chunks, valid, qscale):
    c_rows = chunk
    heads, dk, dv = s_ref.shape

    @pl.when(pl.program_id(1) == 0)
    def _():
        s_ref[...] = s0_ref[0]

    row = lax.broadcasted_iota(jnp.int32, (c_rows, c_rows), 0)
    col = lax.broadcasted_iota(jnp.int32, (c_rows, c_rows), 1)
    tril = col <= row
    tri_b = tril.astype(bf16)
    ones_b = jnp.ones((c_rows, 128), bf16)
    wa2 = wa2_ref[...]
    ba = ba_ref[...]
    gn = gn_ref[...]
    tn = (((0,), (0,)), ((), ()))
    nt = (((1,), (1,)), ((), ()))

    def body(c, carry):
        sl = pl.ds(pl.multiple_of(c * c_rows, c_rows), c_rows)
        x = jnp.dot(a1_ref[0, sl, :].astype(bf16), wa2, preferred_element_type=f32) + ba
        la = (jnp.minimum(x, 0.0) - jnp.log1p(jnp.exp(-jnp.abs(x)))) * (1.0 / GLA_TAU)
        if valid < c_rows:
            la = jnp.where(lax.broadcasted_iota(jnp.int32, la.shape, 0) < valid, la, 0.0)
        parts = _split3(la)
        b = sum(jnp.dot(tri_b, p, preferred_element_type=f32) for p in parts)
        b_last = b[c_rows - 1:c_rows, :]
        q = q_ref[0, sl, :] * qscale
        k = k_ref[0, sl, :]
        vb = v_ref[0, sl, :].astype(bf16)
        r = r_ref[0, sl, :]
        qd = (q * jnp.exp(b)).astype(bf16)
        ki = (k * jnp.exp(-b)).astype(bf16)
        ke = (k * jnp.exp(b_last - b)).astype(bf16)
        for h in range(heads):
            kc = slice(h * dk, (h + 1) * dk)
            vc = slice(h * dv, (h + 1) * dv)
            att = lax.dot_general(qd[:, kc], ki[:, kc], nt, preferred_element_type=f32)
            att = jnp.where(tril, att, 0.0).astype(bf16)
            s_old = s_ref[h]
            o = (jnp.dot(att, vb[:, vc], preferred_element_type=f32)
                 + jnp.dot(qd[:, kc], s_old.astype(bf16), preferred_element_type=f32))
            dl_col = sum(lax.dot_general(p[:, kc], ones_b, tn, preferred_element_type=f32) for p in parts)
            decay = jnp.concatenate([jnp.exp(dl_col)] * (dv // 128), axis=1)
            s_ref[h] = s_old * decay + lax.dot_general(ke[:, kc], vb[:, vc], tn, preferred_element_type=f32)
            o = o * lax.rsqrt(jnp.mean(o * o, axis=-1, keepdims=True) + EPS) * gn[:, vc]
            rh = r[:, vc]
            og_ref[0, sl, vc] = (o * (rh * jax.nn.sigmoid(rh))).astype(og_ref.dtype)
        return carry

    lax.fori_loop(0, n_chunks, body, 0)

    @pl.when(pl.program_id(1) == pl.num_programs(1) - 1)
    def _():
        sout_ref[0] = s_ref[...]


def _gla(proj, a1, wa2, ba, gn, s0, layer, tb, valid):
    bsz, t, _ = proj.shape
    heads = GLA_HEADS
    dk, dv = s0.shape[-2], s0.shape[-1]
    kd, vd = heads * dk, heads * dv
    assert vd == 2 * kd
    kern = functools.partial(_gla_kernel, chunk=GLA_CHUNK, n_chunks=tb // GLA_CHUNK, valid=valid,
                             qscale=float(dk) ** -0.5)
    return pl.pallas_call(
        kern,
        grid=(bsz, t // tb),
        in_specs=[
            pl.BlockSpec((1, tb, kd), lambda b, i: (b, i, 0)),
            pl.BlockSpec((1, tb, kd), lambda b, i: (b, i, 1)),
            pl.BlockSpec((1, tb, vd), lambda b, i: (b, i, 1)),
            pl.BlockSpec((1, tb, vd), lambda b, i: (b, i, 2)),
            pl.BlockSpec((1, tb, GLA_RANK_PAD), lambda b, i: (b, i, 0)),
            pl.BlockSpec((None, GLA_RANK_PAD, kd), lambda b, i: (layer, 0, 0)),
            pl.BlockSpec((None, 1, kd), lambda b, i: (layer, 0, 0)),
            pl.BlockSpec((None, 1, vd), lambda b, i: (layer, 0, 0)),
            pl.BlockSpec((1, heads, dk, dv), lambda b, i: (b, 0, 0, 0)),
        ],
        out_specs=[
            pl.BlockSpec((1, tb, vd), lambda b, i: (b, i, 0)),
            pl.BlockSpec((1, heads, dk, dv), lambda b, i: (b, 0, 0, 0)),
        ],
        out_shape=[
            jax.ShapeDtypeStruct((bsz, t, vd), bf16),
            jax.ShapeDtypeStruct((bsz, heads, dk, dv), f32),
        ],
        scratch_shapes=[pltpu.VMEM((heads, dk, dv), f32)],
        compiler_params=_cparams(("parallel", "arbitrary")),
        name="gla",
    )(proj, proj, proj, proj, a1, wa2, ba, gn, s0)


def _softplus_both(z):
    sp = jnp.maximum(z, 0.0) + jnp.log(1.0 + jnp.exp(-jnp.abs(z)))
    return sp, sp - z


def _sbp_kernel(q_ref, k_ref, v_ref, bias_ref, o_ref, *, bq, hd, scale):
    i = pl.program_id(2)
    nh = q_ref.shape[-1] // hd
    qs = [(q_ref[0, :, h * hd:(h + 1) * hd] * scale).astype(bf16) for h in range(nh)]
    biases = [bias_ref[h] for h in range(nh)]
    sub = SB_SUB
    rj = lax.broadcasted_iota(jnp.int32, (sub, 2 * sub), 0)
    cs = lax.broadcasted_iota(jnp.int32, (sub, 2 * sub), 1)
    ntri = jnp.where((cs >= sub) | (rj > cs), -1.0, 0.0).astype(bf16)
    col_minus_row = (lax.broadcasted_iota(jnp.int32, (bq, SB_KEYS), 1)
                     - lax.broadcasted_iota(jnp.int32, (bq, SB_KEYS), 0))
    nt = (((1,), (1,)), ((), ()))
    steps_per_q = bq // SB_KEYS

    def step(h, kb, carry, acc, mask_offset):
        ks = pl.ds(pl.multiple_of(kb * SB_KEYS, SB_KEYS), SB_KEYS)
        k = k_ref[0, ks, h * hd:(h + 1) * hd].astype(bf16)
        v = v_ref[0, ks, h * hd:(h + 1) * hd].astype(bf16)
        z = lax.dot_general(qs[h], k, nt, preferred_element_type=f32) + biases[h]
        sp, spn = _softplus_both(z)
        if mask_offset is not None:
            past = col_minus_row < mask_offset
            sp = jnp.where(past, sp, 0.0)
        spb = sp.astype(bf16)
        r_hi = jnp.dot(spb[:, sub:], ntri, preferred_element_type=f32)
        r_lo = jnp.dot(spb[:, :sub], ntri, preferred_element_type=f32)
        c_mid = carry + r_hi[:, sub:]
        after = jnp.concatenate([c_mid + r_lo[:, :sub], carry + r_hi[:, :sub]], axis=1)
        a = jnp.exp(after - spn)
        if mask_offset is not None:
            a = jnp.where(past, a, 0.0)
        acc = acc + jnp.dot(a.astype(bf16), v, preferred_element_type=f32)
        return c_mid + r_lo[:, sub:], acc

    state = []
    for h in range(nh):
        carry = jnp.zeros((bq, sub), f32)
        acc = jnp.zeros((bq, hd), f32)
        for d in range(steps_per_q - 1, -1, -1):
            carry, acc = step(h, i * steps_per_q + d, carry, acc, -d * SB_KEYS)
        state += [carry, acc]

    def body(n, st):
        kb = i * steps_per_q - 1 - 2 * n
        out = []
        for h in range(nh):
            c, a = step(h, kb, st[2 * h], st[2 * h + 1], None)
            out += list(step(h, kb - 1, c, a, None))
        return tuple(out)

    state = lax.fori_loop(0, i * (steps_per_q // 2), body, tuple(state))
    for h in range(nh):
        o_ref[0, :, h * hd:(h + 1) * hd] = state[2 * h + 1].astype(o_ref.dtype)


def _sb_prompt(q, k, v, bias, heads, bq=512, nh=4):
    bsz, t, d = q.shape
    hd = d // heads
    assert bq % (2 * SB_KEYS) == 0 and t % bq == 0 and heads % nh == 0
    kern = functools.partial(_sbp_kernel, bq=bq, hd=hd, scale=float(hd) ** -0.5)
    return pl.pallas_call(
        kern,
        grid=(bsz, heads // nh, t // bq),
        in_specs=[
            pl.BlockSpec((1, bq, nh * hd), lambda b, h, i: (b, i, h)),
            pl.BlockSpec((1, t, nh * hd), lambda b, h, i: (b, 0, h)),
            pl.BlockSpec((1, t, nh * hd), lambda b, h, i: (b, 0, h)),
            pl.BlockSpec((nh, 1, SB_KEYS), lambda b, h, i: (h, 0, 0)),
        ],
        out_specs=pl.BlockSpec((1, bq, nh * hd), lambda b, h, i: (b, i, h)),
        out_shape=jax.ShapeDtypeStruct((bsz, t, d), bf16),
        compiler_params=_cparams(("parallel", "parallel", "arbitrary")),
        name="sb_prompt",
    )(q, k, v, bias)


def _sbs_kernel(pt_ref, q_ref, kn_ref, vn_ref, *rest, heads, nq, scale, npp):
    k_refs, v_refs = rest[:npp], rest[npp:2 * npp]
    bias_ref, o_ref, qm_ref, acc_ref, carry_ref = rest[2 * npp:]
    p = pl.program_id(1)
    hd = k_refs[0].shape[2]
    ps = k_refs[0].shape[1] // heads
    d = heads * hd
    nc = heads * nq
    bias = bias_ref[...]
    rs = lax.broadcasted_iota(jnp.int32, (2 * ps, 2 * ps), 0)
    cj = lax.broadcasted_iota(jnp.int32, (2 * ps, 2 * ps), 1)
    tri = (cj > rs).astype(bf16)
    nt = (((1,), (1,)), ((), ()))
    tn = (((0,), (0,)), ((), ()))

    def gather(ref):
        return jnp.concatenate(
            [ref[0, pl.ds(h, ps, stride=heads), :].astype(bf16) for h in range(heads)], axis=1)

    def weights(kp, carry, tri_m, masked):
        n = kp.shape[0]
        z = lax.dot_general(kp, qm_ref[...], nt, preferred_element_type=f32) * scale + bias
        sp, spn = _softplus_both(z)
        if masked:
            key = lax.broadcasted_iota(jnp.int32, (n, nc), 0)
            qry = lax.broadcasted_iota(jnp.int32, (n, nc), 1) % nq
            past = key < qry
            sp = jnp.where(past, sp, 0.0)
        after = carry - jnp.dot(tri_m, sp.astype(bf16), preferred_element_type=f32)
        w = jnp.exp(after - spn)
        if masked:
            w = jnp.where(past, w, 0.0)
        return w.astype(bf16), carry - jnp.sum(sp, axis=0, keepdims=True)

    @pl.when(p == 0)
    def _():
        q = q_ref[0]
        qrep = jnp.concatenate([q] * heads, axis=0)
        rh = lax.broadcasted_iota(jnp.int32, (nc, d), 0) // nq
        ch = lax.broadcasted_iota(jnp.int32, (nc, d), 1) // hd
        qm_ref[...] = jnp.where(rh == ch, qrep, 0.0).astype(bf16)
        w, carry = weights(kn_ref[0].astype(bf16), jnp.zeros((1, nc), f32), tri[:ps, :ps], True)
        acc_ref[...] = lax.dot_general(w, vn_ref[0].astype(bf16), tn, preferred_element_type=f32)
        carry_ref[...] = jnp.broadcast_to(carry, carry_ref.shape)

    carry = carry_ref[0:1, :]
    ws = []
    for g in range(0, npp, 2):
        kp = jnp.concatenate([gather(k_refs[g + 1]), gather(k_refs[g])], axis=0)
        w, carry = weights(kp, carry, tri, False)
        ws.append(w)
    w_all = jnp.concatenate(ws[::-1], axis=0)
    v_all = jnp.concatenate([gather(r) for r in v_refs[::-1]], axis=0)
    acc_ref[...] += lax.dot_general(w_all, v_all, tn, preferred_element_type=f32)
    carry_ref[...] = jnp.broadcast_to(carry, carry_ref.shape)

    @pl.when(p == pl.num_programs(1) - 1)
    def _():
        for h in range(heads):
            o_ref[0, :, h * hd:(h + 1) * hd] = acc_ref[h * nq:(h + 1) * nq, h * hd:(h + 1) * hd].astype(o_ref.dtype)


def _sb_sample(q, k_new, v_new, cache_k, cache_v, page_table, bias, layer, heads, npp=4):
    bsz, nq, d = q.shape
    n_pages = page_table.shape[1]
    assert n_pages % npp == 0 and npp % 2 == 0
    hd = d // heads
    ps = cache_k.shape[2] // heads
    nc = heads * nq
    kern = functools.partial(_sbs_kernel, heads=heads, nq=nq, scale=float(hd) ** -0.5, npp=npp)
    cache_spec = lambda back: pl.BlockSpec(
        (None, 1, ps * heads, hd), lambda b, p, pt: (layer, pt[b, n_pages - back - npp * p], 0, 0))
    cache_specs = [cache_spec(r + 1) for r in range(npp)]
    grid_spec = pltpu.PrefetchScalarGridSpec(
        num_scalar_prefetch=1,
        grid=(bsz, n_pages // npp),
        in_specs=[
            pl.BlockSpec((1, nq, d), lambda b, p, pt: (b, 0, 0)),
            pl.BlockSpec((1, ps, d), lambda b, p, pt: (b, 0, 0)),
            pl.BlockSpec((1, ps, d), lambda b, p, pt: (b, 0, 0)),
            *cache_specs, *cache_specs,
            pl.BlockSpec((1, nc), lambda b, p, pt: (0, 0)),
        ],
        out_specs=pl.BlockSpec((1, nq, d), lambda b, p, pt: (b, 0, 0)),
        scratch_shapes=[
            pltpu.VMEM((nc, d), bf16),
            pltpu.VMEM((nc, d), f32),
            pltpu.VMEM((8, nc), f32),
        ],
    )
    return pl.pallas_call(
        kern,
        grid_spec=grid_spec,
        out_shape=jax.ShapeDtypeStruct((bsz, nq, d), bf16),
        compiler_params=_cparams(("parallel", "arbitrary")),
        name="sb_sample",
    )(page_table, q, k_new, v_new, *([cache_k] * npp), *([cache_v] * npp), bias)


def _pool_kernel(*refs, bm, has_halo, pos0, windows):
    if has_halo:
        x_ref, xh_ref, st_ref, g_ref, w_ref, sc_ref, o_ref, so_ref, ext_ref = refs
    else:
        x_ref, st_ref, g_ref, w_ref, sc_ref, o_ref, so_ref, ext_ref = refs
    i = pl.program_id(1)
    g = g_ref[...]
    x = x_ref[0]
    h = _rms(x, g)
    if has_halo:
        halo = jnp.where(i == 0, st_ref[0], _rms(xh_ref[0], g))
    else:
        halo = st_ref[0]
    ext_ref[0:POOL_HALO, :] = halo
    ext_ref[POOL_HALO:POOL_HALO + bm, :] = h
    gd = x.shape[-1] // len(windows)
    pos = pos0 + i * bm + lax.broadcasted_iota(jnp.int32, (bm, 1), 0)
    for gi, w in enumerate(windows):
        cs = slice(gi * gd, (gi + 1) * gd)
        hg = h[:, cs]
        win = hg
        for s in range(1, w):
            win = win + ext_ref[pl.ds(POOL_HALO - s, bm), cs]
        cnt = jnp.minimum(w, pos + 1).astype(f32)
        dlt = (win / cnt - hg).astype(bf16)
        y = jnp.dot(dlt, w_ref[gi], preferred_element_type=f32)
        o_ref[0, :, cs] = x[:, cs] + y * sc_ref[:, cs]
    so_ref[0] = ext_ref[pl.ds(bm, POOL_HALO), :]


def _pool(x, state, g, w, sc, layer, wlayer, bm, pos0):
    bsz, t, d = x.shape
    ng, gd = w.shape[1], w.shape[2]
    has_halo = t > bm
    hb = bm // POOL_HALO
    kern = functools.partial(_pool_kernel, bm=bm, has_halo=has_halo, pos0=pos0, windows=POOL_WINDOWS)
    in_specs = [pl.BlockSpec((1, bm, d), lambda b, i: (b, i, 0))]
    args = [x]
    if has_halo:
        in_specs.append(pl.BlockSpec((1, POOL_HALO, d), lambda b, i: (b, jnp.maximum(i * hb - 1, 0), 0)))
        args.append(x)
    in_specs += [
        pl.BlockSpec((1, POOL_HALO, d), lambda b, i: (b, 0, 0)),
        pl.BlockSpec((None, 1, d), lambda b, i: (layer, 0, 0)),
        pl.BlockSpec((None, ng, gd, gd), lambda b, i: (wlayer, 0, 0, 0)),
        pl.BlockSpec((None, 1, d), lambda b, i: (wlayer, 0, 0)),
    ]
    args += [state, g, w, sc]
    return pl.pallas_call(
        kern,
        grid=(bsz, t // bm),
        in_specs=in_specs,
        out_specs=[
            pl.BlockSpec((1, bm, d), lambda b, i: (b, i, 0)),
            pl.BlockSpec((1, POOL_HALO, d), lambda b, i: (b, 0, 0)),
        ],
        out_shape=[
            jax.ShapeDtypeStruct((bsz, t, d), f32),
            jax.ShapeDtypeStruct((bsz, POOL_HALO, d), f32),
        ],
        scratch_shapes=[pltpu.VMEM((POOL_HALO + bm, d), f32)],
        compiler_params=_cparams(("parallel", "arbitrary")),
        name="pool",
    )(*args)


def kernel(x_prompt, x_sample, state_gla, cache_sb_k, cache_sb_v, state_pool, page_table, p_prompt, p_sample, norm_ffn1, ffn1_gate, ffn1_up, ffn1_down, norm_mix, norm_ffn2, ffn2_gate, ffn2_up, ffn2_down, norm_ple, ple_gate, ple_proj, gla_in, gla_a1, gla_a2, gla_a_bias, gla_norm, gla_out, sb_in, sb_bias, sb_out, pool_w, pool_scale, norm_final):
    bp, t, d = x_prompt.shape
    bs, ts, _ = x_sample.shape
    depth = norm_ffn1.shape[0]
    mp, ms = bp * t, bs * ts
    kd = gla_a2.shape[-1]
    dk = kd // GLA_HEADS
    dv = gla_out.shape[1] // GLA_HEADS
    hd = d // SB_HEADS
    page = cache_sb_k.shape[2]
    pstate = state_pool.shape[2]

    row3 = lambda a: a.reshape(a.shape[0], 1, a.shape[-1])
    n_ffn1, n_mix, n_ffn2, n_ple = row3(norm_ffn1), row3(norm_mix), row3(norm_ffn2), row3(norm_ple)
    n_final = norm_final.reshape(1, d)
    w1g, w1u, w1d = ffn1_gate.astype(bf16), ffn1_up.astype(bf16), ffn1_down.astype(bf16)
    w2g, w2u, w2d = ffn2_gate.astype(bf16), ffn2_up.astype(bf16), ffn2_down.astype(bf16)
    wpg, wpp = ple_gate.astype(bf16), ple_proj.astype(bf16)
    n_gla = gla_in.shape[0]
    gla_w = gla_in.astype(bf16)
    gla_wa1 = jnp.pad(gla_a1, ((0, 0), (0, 0), (0, GLA_RANK_PAD - gla_a1.shape[-1]))).astype(bf16)
    gla_wa2 = jnp.concatenate(
        [gla_a2, jnp.zeros((n_gla, GLA_RANK_PAD - gla_a2.shape[1], kd), f32)], axis=1).astype(bf16)
    gla_ba = row3(gla_a_bias)
    gla_gn = gla_norm.reshape(n_gla, 1, GLA_HEADS * dv)
    gla_wo = gla_out.astype(bf16)
    sb_wi, sb_wo = sb_in.astype(bf16), sb_out.astype(bf16)
    pool_wb = pool_w.astype(bf16)
    pool_sc = row3(pool_scale)
    pp = p_prompt.reshape(depth, mp, -1)
    psm = p_sample.reshape(depth, ms, -1)

    xp = x_prompt.reshape(mp, d)
    xs = x_sample.reshape(ms, d)
    gla_p, gla_s, kp, vp, ksm, vsm, pool_p, pool_s = [], [], [], [], [], [], [], []
    for i in range(depth):
        kind, j = i % N_MIXERS, i // N_MIXERS
        xp, xs = _ffn(xp, xs, n_ffn1, w1g, w1u, w1d, i, bm=FFN_ROWS)
        if kind == 0:
            proj_p, a1_p = _norm_mm(xp, n_mix, gla_w, i, j, bm=1024, bn=512, w2=gla_wa1)
            proj_s, a1_s = _norm_mm(xs, n_mix, gla_w, i, j, bm=ms, bn=512, w2=gla_wa1)
            pad_s = lambda a: jnp.pad(a.reshape(bs, ts, -1), ((0, 0), (0, GLA_CHUNK - ts), (0, 0)))
            s0 = jnp.zeros((bp, GLA_HEADS, dk, dv), f32)
            og_p, st_p = _gla(proj_p.reshape(bp, t, -1), a1_p.reshape(bp, t, -1), gla_wa2, gla_ba, gla_gn,
                              s0, j, tb=256, valid=GLA_CHUNK)
            og_s, st_s = _gla(pad_s(proj_s), pad_s(a1_s), gla_wa2, gla_ba, gla_gn,
                              state_gla[j], j, tb=GLA_CHUNK, valid=ts)
            xp = _mm_res(og_p.reshape(mp, -1), gla_wo, xp, j, bm=1024, bn=1024)
            xs = _mm_res(og_s[:, :ts].reshape(ms, -1), gla_wo, xs, j, bm=ms, bn=1024)
            gla_p.append(st_p)
            gla_s.append(st_s)
        elif kind == 1:
            q_p, k_p, v_p = [a.reshape(bp, t, d)
                             for a in _norm_mm_split(xp, n_mix, sb_wi, i, j, bm=1024, bn=512, nsplit=3)]
            qkv_s = _norm_mm(xs, n_mix, sb_wi, i, j, bm=ms, bn=512)
            bias_p = jnp.broadcast_to(sb_bias[j][:, None, None], (SB_HEADS, 1, SB_KEYS))
            o_p = _sb_prompt(q_p, k_p, v_p, bias_p, SB_HEADS)
            k_s = qkv_s[:, d:2 * d].reshape(bs, ts, d)
            v_s = qkv_s[:, 2 * d:].reshape(bs, ts, d)
            pad = ((0, 0), (0, page - ts), (0, 0))
            bias_s = jnp.repeat(sb_bias[j], ts)[None, :]
            o_s = _sb_sample(qkv_s[:, :d].reshape(bs, ts, d), jnp.pad(k_s, pad), jnp.pad(v_s, pad),
                             cache_sb_k.reshape(cache_sb_k.shape[:2] + (page * SB_HEADS, hd)),
                             cache_sb_v.reshape(cache_sb_v.shape[:2] + (page * SB_HEADS, hd)),
                             page_table, bias_s, j, SB_HEADS)
            xp = _mm_res(o_p.reshape(mp, d), sb_wo, xp, j, bm=1024, bn=1024)
            xs = _mm_res(o_s.reshape(ms, d), sb_wo, xs, j, bm=ms, bn=1024)
            kp.append(k_p.reshape(bp, t, SB_HEADS, hd))
            vp.append(v_p.reshape(bp, t, SB_HEADS, hd))
            ksm.append(k_s.reshape(bs, ts, SB_HEADS, hd))
            vsm.append(v_s.reshape(bs, ts, SB_HEADS, hd))
        else:
            st0_p = jnp.zeros((bp, POOL_HALO, d), f32)
            st0_s = jnp.pad(state_pool[j], ((0, 0), (POOL_HALO - pstate, 0), (0, 0)))
            xp3, so_p = _pool(xp.reshape(bp, t, d), st0_p, n_mix, pool_wb, pool_sc, i, j, bm=512, pos0=0)
            xs3, so_s = _pool(xs.reshape(bs, ts, d), st0_s, n_mix, pool_wb, pool_sc, i, j, bm=ts,
                              pos0=page_table.shape[1] * page)
            xp, xs = xp3.reshape(mp, d), xs3.reshape(ms, d)
            pool_p.append(so_p[:, POOL_HALO - pstate:])
            pool_s.append(so_s[:, POOL_HALO - pstate:])
        xp, xs = _ffn(xp, xs, n_ffn2, w2g, w2u, w2d, i, bm=FFN_ROWS)
        final = i == depth - 1
        xp = _ple(xp, pp, n_ple, wpg, wpp, n_final, i, bm=512, final=final)
        xs = _ple(xs, psm, n_ple, wpg, wpp, n_final, i, bm=ms, final=final)
    return (xp.reshape(bp, t, d), xs.reshape(bs, ts, d), jnp.stack(gla_p), jnp.stack(gla_s),
            jnp.stack(kp), jnp.stack(vp), jnp.stack(ksm), jnp.stack(vsm),
            jnp.stack(pool_p), jnp.stack(pool_s))
```

```python
import functools

import jax
import jax.numpy as jnp
from jax import lax
from jax.experimental import pallas as pl
from jax.experimental.pallas import tpu as pltpu

f32 = jnp.float32
bf16 = jnp.bfloat16

EPS = 1e-6
N_MIXERS = 3
GLA_HEADS = 4
GLA_TAU = 16.0
GLA_CHUNK = 64
GLA_RANK_PAD = 128
SB_HEADS = 16
SB_KEYS = 256
SB_SUB = 128
POOL_WINDOWS = (2, 4, 8, 16)
POOL_HALO = 16
VMEM_LIMIT = 58 * 1024 * 1024
FFN_ROWS = 512


def _cparams(sem):
    return pltpu.CompilerParams(dimension_semantics=sem, vmem_limit_bytes=VMEM_LIMIT)


def _rms(x, g):
    return x * lax.rsqrt(jnp.mean(x * x, axis=-1, keepdims=True) + EPS) * g


def _ffn_kernel(x_ref, xs_ref, g_ref, wg_ref, wu_ref, wd_ref, o_ref, os_ref, h_ref):
    i = pl.program_id(0)
    j = pl.program_id(1)
    bm = x_ref.shape[0]
    ms = xs_ref.shape[0]

    def swiglu_down(h):
        a = jnp.dot(h, wg_ref[...].astype(bf16), preferred_element_type=f32)
        u = jnp.dot(h, wu_ref[...].astype(bf16), preferred_element_type=f32)
        act = (0.5 * a * jax.nn.sigmoid(a) * u).astype(bf16)
        return jnp.dot(act, wd_ref[...].astype(bf16), preferred_element_type=f32)

    @pl.when(j == 0)
    def _():
        x = x_ref[...]
        h_ref[0:bm, :] = _rms(x, g_ref[...]).astype(bf16)
        o_ref[...] = x

    @pl.when((j == 0) & (i == 0))
    def _():
        xs = xs_ref[...]
        h_ref[bm:bm + ms, :] = _rms(xs, g_ref[...]).astype(bf16)
        os_ref[...] = xs

    @pl.when(i == 0)
    def _():
        y = swiglu_down(h_ref[...])
        o_ref[...] += y[:bm]
        os_ref[...] += y[bm:]

    @pl.when(i != 0)
    def _():
        o_ref[...] += swiglu_down(h_ref[0:bm, :])


def _ffn(x, xs, g, wg, wu, wd, layer, bm, bf=512):
    m, d = x.shape
    ms = xs.shape[0]
    f = wg.shape[-1]
    return pl.pallas_call(
        _ffn_kernel,
        grid=(m // bm, f // bf),
        in_specs=[
            pl.BlockSpec((bm, d), lambda i, j: (i, 0)),
            pl.BlockSpec((ms, d), lambda i, j: (0, 0)),
            pl.BlockSpec((None, 1, d), lambda i, j: (layer, 0, 0)),
            pl.BlockSpec((None, d, bf), lambda i, j: (layer, 0, j)),
            pl.BlockSpec((None, d, bf), lambda i, j: (layer, 0, j)),
            pl.BlockSpec((None, bf, d), lambda i, j: (layer, j, 0)),
        ],
        out_specs=[
            pl.BlockSpec((bm, d), lambda i, j: (i, 0)),
            pl.BlockSpec((ms, d), lambda i, j: (0, 0)),
        ],
        out_shape=[jax.ShapeDtypeStruct((m, d), f32), jax.ShapeDtypeStruct((ms, d), f32)],
        scratch_shapes=[pltpu.VMEM((bm + ms, d), bf16)],
        compiler_params=_cparams(("arbitrary", "arbitrary")),
        name="ffn",
    )(x, xs, g, wg, wu, wd)


def _ple_kernel(x_ref, p_ref, g_ref, wg_ref, wp_ref, gf_ref, o_ref, *, final):
    x = x_ref[...]
    h = _rms(x, g_ref[...]).astype(bf16)
    gate = jax.nn.sigmoid(jnp.dot(h, wg_ref[...], preferred_element_type=f32))
    proj = jnp.dot(p_ref[...].astype(bf16), wp_ref[...], preferred_element_type=f32)
    y = x + gate * proj
    if final:
        y = _rms(y, gf_ref[...])
    o_ref[...] = y


def _ple(x, p, g, wg, wp, gf, layer, bm, final):
    m, d = x.shape
    pd = p.shape[-1]
    return pl.pallas_call(
        functools.partial(_ple_kernel, final=final),
        grid=(m // bm,),
        in_specs=[
            pl.BlockSpec((bm, d), lambda i: (i, 0)),
            pl.BlockSpec((None, bm, pd), lambda i: (layer, i, 0)),
            pl.BlockSpec((None, 1, d), lambda i: (layer, 0, 0)),
            pl.BlockSpec((None, d, d), lambda i: (layer, 0, 0), pipeline_mode=pl.Buffered(1)),
            pl.BlockSpec((None, pd, d), lambda i: (layer, 0, 0), pipeline_mode=pl.Buffered(1)),
            pl.BlockSpec((1, d), lambda i: (0, 0)),
        ],
        out_specs=pl.BlockSpec((bm, d), lambda i: (i, 0)),
        out_shape=jax.ShapeDtypeStruct((m, d), f32),
        compiler_params=_cparams(("parallel",)),
        name="ple",
    )(x, p, g, wg, wp, gf)


def _norm_mm_kernel(*refs, has_aux):
    if has_aux:
        x_ref, g_ref, w_ref, w2_ref, o_ref, o2_ref, h_ref = refs
    else:
        x_ref, g_ref, w_ref, o_ref, h_ref = refs

    @pl.when(pl.program_id(1) == 0)
    def _():
        h_ref[...] = _rms(x_ref[...], g_ref[...]).astype(bf16)
        if has_aux:
            o2_ref[...] = jnp.dot(h_ref[...], w2_ref[...], preferred_element_type=f32)

    o_ref[...] = jnp.dot(h_ref[...], w_ref[...], preferred_element_type=f32)


def _norm_mm_split_kernel(x_ref, g_ref, w_ref, *rest, per):
    outs, h_ref = rest[:-1], rest[-1]
    j = pl.program_id(1)

    @pl.when(j == 0)
    def _():
        h_ref[...] = _rms(x_ref[...], g_ref[...]).astype(bf16)

    y = jnp.dot(h_ref[...], w_ref[...], preferred_element_type=f32)
    for s, o_ref in enumerate(outs):
        @pl.when((j >= s * per) & (j < (s + 1) * per))
        def _(o_ref=o_ref):
            o_ref[...] = y


def _norm_mm_split(x, g, w, layer, wlayer, bm, bn, nsplit):
    m, d = x.shape
    n = w.shape[-1]
    per = n // nsplit // bn
    assert n == nsplit * per * bn and m % bm == 0
    return pl.pallas_call(
        functools.partial(_norm_mm_split_kernel, per=per),
        grid=(m // bm, n // bn),
        in_specs=[
            pl.BlockSpec((bm, d), lambda i, j: (i, 0)),
            pl.BlockSpec((None, 1, d), lambda i, j: (layer, 0, 0)),
            pl.BlockSpec((None, d, bn), lambda i, j: (wlayer, 0, j)),
        ],
        out_specs=[pl.BlockSpec((bm, bn), lambda i, j, s=s: (i, jnp.clip(j - s * per, 0, per - 1)))
                   for s in range(nsplit)],
        out_shape=[jax.ShapeDtypeStruct((m, n // nsplit), f32)] * nsplit,
        scratch_shapes=[pltpu.VMEM((bm, d), bf16)],
        compiler_params=_cparams(("arbitrary", "arbitrary")),
        name="norm_mm_split",
    )(x, g, w)


def _norm_mm(x, g, w, layer, wlayer, bm, bn, w2=None):
    m, d = x.shape
    n = w.shape[-1]
    assert n % bn == 0 and m % bm == 0
    in_specs = [
        pl.BlockSpec((bm, d), lambda i, j: (i, 0)),
        pl.BlockSpec((None, 1, d), lambda i, j: (layer, 0, 0)),
        pl.BlockSpec((None, d, bn), lambda i, j: (wlayer, 0, j)),
    ]
    out_specs = [pl.BlockSpec((bm, bn), lambda i, j: (i, j))]
    out_shape = [jax.ShapeDtypeStruct((m, n), f32)]
    args = [x, g, w]
    if w2 is not None:
        n2 = w2.shape[-1]
        in_specs.append(pl.BlockSpec((None, d, n2), lambda i, j: (wlayer, 0, 0)))
        out_specs.append(pl.BlockSpec((bm, n2), lambda i, j: (i, 0)))
        out_shape.append(jax.ShapeDtypeStruct((m, n2), f32))
        args.append(w2)
    outs = pl.pallas_call(
        functools.partial(_norm_mm_kernel, has_aux=w2 is not None),
        grid=(m // bm, n // bn),
        in_specs=in_specs,
        out_specs=out_specs,
        out_shape=out_shape,
        scratch_shapes=[pltpu.VMEM((bm, d), bf16)],
        compiler_params=_cparams(("parallel", "arbitrary")),
        name="norm_mm",
    )(*args)
    return outs if w2 is not None else outs[0]


def _mm_res_kernel(a_ref, w_ref, x_ref, o_ref):
    o_ref[...] = x_ref[...] + jnp.dot(a_ref[...], w_ref[...], preferred_element_type=f32)


def _mm_res(a, w, x, wlayer, bm, bn):
    m, k = a.shape
    n = w.shape[-1]
    return pl.pallas_call(
        _mm_res_kernel,
        grid=(m // bm, n // bn),
        in_specs=[
            pl.BlockSpec((bm, k), lambda i, j: (i, 0)),
            pl.BlockSpec((None, k, bn), lambda i, j: (wlayer, 0, j)),
            pl.BlockSpec((bm, bn), lambda i, j: (i, j)),
        ],
        out_specs=pl.BlockSpec((bm, bn), lambda i, j: (i, j)),
        out_shape=jax.ShapeDtypeStruct((m, n), f32),
        compiler_params=_cparams(("parallel", "arbitrary")),
        name="mm_res",
    )(a, w, x)


def _split3(x):
    hi = x.astype(bf16)
    r1 = x - hi.astype(f32)
    mid = r1.astype(bf16)
    lo = (r1 - mid.astype(f32)).astype(bf16)
    return hi, mid, lo


def _gla_kernel(q_ref, k_ref, v_ref, r_ref, a1_ref, wa2_ref, ba_ref, gn_ref, s0_ref,
                og_ref, sout_ref, s_ref, *, chunk, n_chunks, valid, qscale):
    c_rows = chunk
    heads, dk, dv = s_ref.shape

    @pl.when(pl.program_id(1) == 0)
    def _():
        s_ref[...] = s0_ref[0]

    row = lax.broadcasted_iota(jnp.int32, (c_rows, c_rows), 0)
    col = lax.broadcasted_iota(jnp.int32, (c_rows, c_rows), 1)
    tril = col <= row
    tri_b = tril.astype(bf16)
    ones_b = jnp.ones((c_rows, 128), bf16)
    wa2 = wa2_ref[...]
    ba = ba_ref[...]
    gn = gn_ref[...]
    tn = (((0,), (0,)), ((), ()))
    nt = (((1,), (1,)), ((), ()))

    def body(c, carry):
        sl = pl.ds(pl.multiple_of(c * c_rows, c_rows), c_rows)
        x = jnp.dot(a1_ref[0, sl, :].astype(bf16), wa2, preferred_element_type=f32) + ba
        la = (jnp.minimum(x, 0.0) - jnp.log1p(jnp.exp(-jnp.abs(x)))) * (1.0 / GLA_TAU)
        if valid < c_rows:
            la = jnp.where(lax.broadcasted_iota(jnp.int32, la.shape, 0) < valid, la, 0.0)
        parts = _split3(la)
        b = sum(jnp.dot(tri_b, p, preferred_element_type=f32) for p in parts)
        b_last = b[c_rows - 1:c_rows, :]
        q = q_ref[0, sl, :] * qscale
        k = k_ref[0, sl, :]
        vb = v_ref[0, sl, :].astype(bf16)
        r = r_ref[0, sl, :]
        qd = (q * jnp.exp(b)).astype(bf16)
        ki = (k * jnp.exp(-b)).astype(bf16)
        ke = (k * jnp.exp(b_last - b)).astype(bf16)
        for h in range(heads):
            kc = slice(h * dk, (h + 1) * dk)
            vc = slice(h * dv, (h + 1) * dv)
            att = lax.dot_general(qd[:, kc], ki[:, kc], nt, preferred_element_type=f32)
            att = jnp.where(tril, att, 0.0).astype(bf16)
            s_old = s_ref[h]
            o = (jnp.dot(att, vb[:, vc], preferred_element_type=f32)
                 + jnp.dot(qd[:, kc], s_old.astype(bf16), preferred_element_type=f32))
            dl_col = sum(lax.dot_general(p[:, kc], ones_b, tn, preferred_element_type=f32) for p in parts)
            decay = jnp.concatenate([jnp.exp(dl_col)] * (dv // 128), axis=1)
            s_ref[h] = s_old * decay + lax.dot_general(ke[:, kc], vb[:, vc], tn, preferred_element_type=f32)
            o = o * lax.rsqrt(jnp.mean(o * o, axis=-1, keepdims=True) + EPS) * gn[:, vc]
            rh = r[:, vc]
            og_ref[0, sl, vc] = (o * (rh * jax.nn.sigmoid(rh))).astype(og_ref.dtype)
        return carry

    lax.fori_loop(0, n_chunks, body, 0)

    @pl.when(pl.program_id(1) == pl.num_programs(1) - 1)
    def _():
        sout_ref[0] = s_ref[...]


def _gla(proj, a1, wa2, ba, gn, s0, layer, tb, valid):
    bsz, t, _ = proj.shape
    heads = GLA_HEADS
    dk, dv = s0.shape[-2], s0.shape[-1]
    kd, vd = heads * dk, heads * dv
    assert vd == 2 * kd
    kern = functools.partial(_gla_kernel, chunk=GLA_CHUNK, n_chunks=tb // GLA_CHUNK, valid=valid,
                             qscale=float(dk) ** -0.5)
    return pl.pallas_call(
        kern,
        grid=(bsz, t // tb),
        in_specs=[
            pl.BlockSpec((1, tb, kd), lambda b, i: (b, i, 0)),
            pl.BlockSpec((1, tb, kd), lambda b, i: (b, i, 1)),
            pl.BlockSpec((1, tb, vd), lambda b, i: (b, i, 1)),
            pl.BlockSpec((1, tb, vd), lambda b, i: (b, i, 2)),
            pl.BlockSpec((1, tb, GLA_RANK_PAD), lambda b, i: (b, i, 0)),
            pl.BlockSpec((None, GLA_RANK_PAD, kd), lambda b, i: (layer, 0, 0)),
            pl.BlockSpec((None, 1, kd), lambda b, i: (layer, 0, 0)),
            pl.BlockSpec((None, 1, vd), lambda b, i: (layer, 0, 0)),
            pl.BlockSpec((1, heads, dk, dv), lambda b, i: (b, 0, 0, 0)),
        ],
        out_specs=[
            pl.BlockSpec((1, tb, vd), lambda b, i: (b, i, 0)),
            pl.BlockSpec((1, heads, dk, dv), lambda b, i: (b, 0, 0, 0)),
        ],
        out_shape=[
            jax.ShapeDtypeStruct((bsz, t, vd), bf16),
            jax.ShapeDtypeStruct((bsz, heads, dk, dv), f32),
        ],
        scratch_shapes=[pltpu.VMEM((heads, dk, dv), f32)],
        compiler_params=_cparams(("parallel", "arbitrary")),
        name="gla",
    )(proj, proj, proj, proj, a1, wa2, ba, gn, s0)


def _softplus_both(z):
    sp = jnp.maximum(z, 0.0) + jnp.log(1.0 + jnp.exp(-jnp.abs(z)))
    return sp, sp - z


def _sbp_kernel(q_ref, k_ref, v_ref, bias_ref, o_ref, *, bq, hd, scale):
    i = pl.program_id(2)
    nh = q_ref.shape[-1] // hd
    qs = [(q_ref[0, :, h * hd:(h + 1) * hd] * scale).astype(bf16) for h in range(nh)]
    biases = [bias_ref[h] for h in range(nh)]
    sub = SB_SUB
    rj = lax.broadcasted_iota(jnp.int32, (sub, 2 * sub), 0)
    cs = lax.broadcasted_iota(jnp.int32, (sub, 2 * sub), 1)
    ntri = jnp.where((cs >= sub) | (rj > cs), -1.0, 0.0).astype(bf16)
    col_minus_row = (lax.broadcasted_iota(jnp.int32, (bq, SB_KEYS), 1)
                     - lax.broadcasted_iota(jnp.int32, (bq, SB_KEYS), 0))
    nt = (((1,), (1,)), ((), ()))
    steps_per_q = bq // SB_KEYS

    def step(h, kb, carry, acc, mask_offset):
        ks = pl.ds(pl.multiple_of(kb * SB_KEYS, SB_KEYS), SB_KEYS)
        k = k_ref[0, ks, h * hd:(h + 1) * hd].astype(bf16)
        v = v_ref[0, ks, h * hd:(h + 1) * hd].astype(bf16)
        z = lax.dot_general(qs[h], k, nt, preferred_element_type=f32) + biases[h]
        sp, spn = _softplus_both(z)
        if mask_offset is not None:
            past = col_minus_row < mask_offset
            sp = jnp.where(past, sp, 0.0)
        spb = sp.astype(bf16)
        r_hi = jnp.dot(spb[:, sub:], ntri, preferred_element_type=f32)
        r_lo = jnp.dot(spb[:, :sub], ntri, preferred_element_type=f32)
        c_mid = carry + r_hi[:, sub:]
        after = jnp.concatenate([c_mid + r_lo[:, :sub], carry + r_hi[:, :sub]], axis=1)
        a = jnp.exp(after - spn)
        if mask_offset is not None:
            a = jnp.where(past, a, 0.0)
        acc = acc + jnp.dot(a.astype(bf16), v, preferred_element_type=f32)
        return c_mid + r_lo[:, sub:], acc

    state = []
    for h in range(nh):
        carry = jnp.zeros((bq, sub), f32)
        acc = jnp.zeros((bq, hd), f32)
        for d in range(steps_per_q - 1, -1, -1):
            carry, acc = step(h, i * steps_per_q + d, carry, acc, -d * SB_KEYS)
        state += [carry, acc]

    def body(n, st):
        kb = i * steps_per_q - 1 - 2 * n
        out = []
        for h in range(nh):
            c, a = step(h, kb, st[2 * h], st[2 * h + 1], None)
            out += list(step(h, kb - 1, c, a, None))
        return tuple(out)

    state = lax.fori_loop(0, i * (steps_per_q // 2), body, tuple(state))
    for h in range(nh):
        o_ref[0, :, h * hd:(h + 1) * hd] = state[2 * h + 1].astype(o_ref.dtype)


def _sb_prompt(q, k, v, bias, heads, bq=512, nh=4):
    bsz, t, d = q.shape
    hd = d // heads
    assert bq % (2 * SB_KEYS) == 0 and t % bq == 0 and heads % nh == 0
    kern = functools.partial(_sbp_kernel, bq=bq, hd=hd, scale=float(hd) ** -0.5)
    return pl.pallas_call(
        kern,
        grid=(bsz, heads // nh, t // bq),
        in_specs=[
            pl.BlockSpec((1, bq, nh * hd), lambda b, h, i: (b, i, h)),
            pl.BlockSpec((1, t, nh * hd), lambda b, h, i: (b, 0, h)),
            pl.BlockSpec((1, t, nh * hd), lambda b, h, i: (b, 0, h)),
            pl.BlockSpec((nh, 1, SB_KEYS), lambda b, h, i: (h, 0, 0)),
        ],
        out_specs=pl.BlockSpec((1, bq, nh * hd), lambda b, h, i: (b, i, h)),
        out_shape=jax.ShapeDtypeStruct((bsz, t, d), bf16),
        compiler_params=_cparams(("parallel", "parallel", "arbitrary")),
        name="sb_prompt",
    )(q, k, v, bias)


def _sbs_kernel(pt_ref, q_ref, kn_ref, vn_ref, *rest, heads, nq, scale, npp):
    k_refs, v_refs = rest[:npp], rest[npp:2 * npp]
    bias_ref, o_ref, qm_ref, acc_ref, carry_ref = rest[2 * npp:]
    p = pl.program_id(1)
    hd = k_refs[0].shape[2]
    ps = k_refs[0].shape[1] // heads
    d = heads * hd
    nc = heads * nq
    bias = bias_ref[...]
    rs = lax.broadcasted_iota(jnp.int32, (2 * ps, 2 * ps), 0)
    cj = lax.broadcasted_iota(jnp.int32, (2 * ps, 2 * ps), 1)
    tri = (cj > rs).astype(bf16)
    nt = (((1,), (1,)), ((), ()))
    tn = (((0,), (0,)), ((), ()))

    def gather(ref):
        return jnp.concatenate(
            [ref[0, pl.ds(h, ps, stride=heads), :].astype(bf16) for h in range(heads)], axis=1)

    def weights(kp, carry, tri_m, masked):
        n = kp.shape[0]
        z = lax.dot_general(kp, qm_ref[...], nt, preferred_element_type=f32) * scale + bias
        sp, spn = _softplus_both(z)
        if masked:
            key = lax.broadcasted_iota(jnp.int32, (n, nc), 0)
            qry = lax.broadcasted_iota(jnp.int32, (n, nc), 1) % nq
            past = key < qry
            sp = jnp.where(past, sp, 0.0)
        after = carry - jnp.dot(tri_m, sp.astype(bf16), preferred_element_type=f32)
        w = jnp.exp(after - spn)
        if masked:
            w = jnp.where(past, w, 0.0)
        return w.astype(bf16), carry - jnp.sum(sp, axis=0, keepdims=True)

    @pl.when(p == 0)
    def _():
        q = q_ref[0]
        qrep = jnp.concatenate([q] * heads, axis=0)
        rh = lax.broadcasted_iota(jnp.int32, (nc, d), 0) // nq
        ch = lax.broadcasted_iota(jnp.int32, (nc, d), 1) // hd
        qm_ref[...] = jnp.where(rh == ch, qrep, 0.0).astype(bf16)
        w, carry = weights(kn_ref[0].astype(bf16), jnp.zeros((1, nc), f32), tri[:ps, :ps], True)
        acc_ref[...] = lax.dot_general(w, vn_ref[0].astype(bf16), tn, preferred_element_type=f32)
        carry_ref[...] = jnp.broadcast_to(carry, carry_ref.shape)

    carry = carry_ref[0:1, :]
    ws = []
    for g in range(0, npp, 2):
        kp = jnp.concatenate([gather(k_refs[g + 1]), gather(k_refs[g])], axis=0)
        w, carry = weights(kp, carry, tri, False)
        ws.append(w)
    w_all = jnp.concatenate(ws[::-1], axis=0)
    v_all = jnp.concatenate([gather(r) for r in v_refs[::-1]], axis=0)
    acc_ref[...] += lax.dot_general(w_all, v_all, tn, preferred_element_type=f32)
    carry_ref[...] = jnp.broadcast_to(carry, carry_ref.shape)

    @pl.when(p == pl.num_programs(1) - 1)
    def _():
        for h in range(heads):
            o_ref[0, :, h * hd:(h + 1) * hd] = acc_ref[h * nq:(h + 1) * nq, h * hd:(h + 1) * hd].astype(o_ref.dtype)


def _sb_sample(q, k_new, v_new, cache_k, cache_v, page_table, bias, layer, heads, npp=4):
    bsz, nq, d = q.shape
    n_pages = page_table.shape[1]
    assert n_pages % npp == 0 and npp % 2 == 0
    hd = d // heads
    ps = cache_k.shape[2] // heads
    nc = heads * nq
    kern = functools.partial(_sbs_kernel, heads=heads, nq=nq, scale=float(hd) ** -0.5, npp=npp)
    cache_spec = lambda back: pl.BlockSpec(
        (None, 1, ps * heads, hd), lambda b, p, pt: (layer, pt[b, n_pages - back - npp * p], 0, 0))
    cache_specs = [cache_spec(r + 1) for r in range(npp)]
    grid_spec = pltpu.PrefetchScalarGridSpec(
        num_scalar_prefetch=1,
        grid=(bsz, n_pages // npp),
        in_specs=[
            pl.BlockSpec((1, nq, d), lambda b, p, pt: (b, 0, 0)),
            pl.BlockSpec((1, ps, d), lambda b, p, pt: (b, 0, 0)),
            pl.BlockSpec((1, ps, d), lambda b, p, pt: (b, 0, 0)),
            *cache_specs, *cache_specs,
            pl.BlockSpec((1, nc), lambda b, p, pt: (0, 0)),
        ],
        out_specs=pl.BlockSpec((1, nq, d), lambda b, p, pt: (b, 0, 0)),
        scratch_shapes=[
            pltpu.VMEM((nc, d), bf16),
            pltpu.VMEM((nc, d), f32),
            pltpu.VMEM((8, nc), f32),
        ],
    )
    return pl.pallas_call(
        kern,
        grid_spec=grid_spec,
        out_shape=jax.ShapeDtypeStruct((bsz, nq, d), bf16),
        compiler_params=_cparams(("parallel", "arbitrary")),
        name="sb_sample",
    )(page_table, q, k_new, v_new, *([cache_k] * npp), *([cache_v] * npp), bias)


def _pool_kernel(*refs, bm, has_halo, pos0, windows):
    if has_halo:
        x_ref, xh_ref, st_ref, g_ref, w_ref, sc_ref, o_ref, so_ref, ext_ref = refs
    else:
        x_ref, st_ref, g_ref, w_ref, sc_ref, o_ref, so_ref, ext_ref = refs
    i = pl.program_id(1)
    g = g_ref[...]
    x = x_ref[0]
    h = _rms(x, g)
    if has_halo:
        halo = jnp.where(i == 0, st_ref[0], _rms(xh_ref[0], g))
    else:
        halo = st_ref[0]
    ext_ref[0:POOL_HALO, :] = halo
    ext_ref[POOL_HALO:POOL_HALO + bm, :] = h
    gd = x.shape[-1] // len(windows)
    pos = pos0 + i * bm + lax.broadcasted_iota(jnp.int32, (bm, 1), 0)
    for gi, w in enumerate(windows):
        cs = slice(gi * gd, (gi + 1) * gd)
        hg = h[:, cs]
        win = hg
        for s in range(1, w):
            win = win + ext_ref[pl.ds(POOL_HALO - s, bm), cs]
        cnt = jnp.minimum(w, pos + 1).astype(f32)
        dlt = (win / cnt - hg).astype(bf16)
        y = jnp.dot(dlt, w_ref[gi], preferred_element_type=f32)
        o_ref[0, :, cs] = x[:, cs] + y * sc_ref[:, cs]
    so_ref[0] = ext_ref[pl.ds(bm, POOL_HALO), :]


def _pool(x, state, g, w, sc, layer, wlayer, bm, pos0):
    bsz, t, d = x.shape
    ng, gd = w.shape[1], w.shape[2]
    has_halo = t > bm
    hb = bm // POOL_HALO
    kern = functools.partial(_pool_kernel, bm=bm, has_halo=has_halo, pos0=pos0, windows=POOL_WINDOWS)
    in_specs = [pl.BlockSpec((1, bm, d), lambda b, i: (b, i, 0))]
    args = [x]
    if has_halo:
        in_specs.append(pl.BlockSpec((1, POOL_HALO, d), lambda b, i: (b, jnp.maximum(i * hb - 1, 0), 0)))
        args.append(x)
    in_specs += [
        pl.BlockSpec((1, POOL_HALO, d), lambda b, i: (b, 0, 0)),
        pl.BlockSpec((None, 1, d), lambda b, i: (layer, 0, 0)),
        pl.BlockSpec((None, ng, gd, gd), lambda b, i: (wlayer, 0, 0, 0)),
        pl.BlockSpec((None, 1, d), lambda b, i: (wlayer, 0, 0)),
    ]
    args += [state, g, w, sc]
    return pl.pallas_call(
        kern,
        grid=(bsz, t // bm),
        in_specs=in_specs,
        out_specs=[
            pl.BlockSpec((1, bm, d), lambda b, i: (b, i, 0)),
            pl.BlockSpec((1, POOL_HALO, d), lambda b, i: (b, 0, 0)),
        ],
        out_shape=[
            jax.ShapeDtypeStruct((bsz, t, d), f32),
            jax.ShapeDtypeStruct((bsz, POOL_HALO, d), f32),
        ],
        scratch_shapes=[pltpu.VMEM((POOL_HALO + bm, d), f32)],
        compiler_params=_cparams(("parallel", "arbitrary")),
        name="pool",
    )(*args)


def kernel(x_prompt, x_sample, state_gla, cache_sb_k, cache_sb_v, state_pool, page_table, p_prompt, p_sample, norm_ffn1, ffn1_gate, ffn1_up, ffn1_down, norm_mix, norm_ffn2, ffn2_gate, ffn2_up, ffn2_down, norm_ple, ple_gate, ple_proj, gla_in, gla_a1, gla_a2, gla_a_bias, gla_norm, gla_out, sb_in, sb_bias, sb_out, pool_w, pool_scale, norm_final):
    bp, t, d = x_prompt.shape
    bs, ts, _ = x_sample.shape
    depth = norm_ffn1.shape[0]
    mp, ms = bp * t, bs * ts
    kd = gla_a2.shape[-1]
    dk = kd // GLA_HEADS
    dv = gla_out.shape[1] // GLA_HEADS
    hd = d // SB_HEADS
    page = cache_sb_k.shape[2]
    pstate = state_pool.shape[2]

    row3 = lambda a: a.reshape(a.shape[0], 1, a.shape[-1])
    n_ffn1, n_mix, n_ffn2, n_ple = row3(norm_ffn1), row3(norm_mix), row3(norm_ffn2), row3(norm_ple)
    n_final = norm_final.reshape(1, d)
    w1g, w1u, w1d = ffn1_gate, ffn1_up, ffn1_down
    w2g, w2u, w2d = ffn2_gate, ffn2_up, ffn2_down
    wpg, wpp = ple_gate.astype(bf16), ple_proj.astype(bf16)
    n_gla = gla_in.shape[0]
    gla_w = gla_in.astype(bf16)
    gla_wa1 = jnp.pad(gla_a1, ((0, 0), (0, 0), (0, GLA_RANK_PAD - gla_a1.shape[-1]))).astype(bf16)
    gla_wa2 = jnp.concatenate(
        [gla_a2, jnp.zeros((n_gla, GLA_RANK_PAD - gla_a2.shape[1], kd), f32)], axis=1).astype(bf16)
    gla_ba = row3(gla_a_bias)
    gla_gn = gla_norm.reshape(n_gla, 1, GLA_HEADS * dv)
    gla_wo = gla_out.astype(bf16)
    sb_wi, sb_wo = sb_in.astype(bf16), sb_out.astype(bf16)
    pool_wb = pool_w.astype(bf16)
    pool_sc = row3(pool_scale)
    pp = p_prompt.reshape(depth, mp, -1)
    psm = p_sample.reshape(depth, ms, -1)

    xp = x_prompt.reshape(mp, d)
    xs = x_sample.reshape(ms, d)
    gla_p, gla_s, kp, vp, ksm, vsm, pool_p, pool_s = [], [], [], [], [], [], [], []
    for i in range(depth):
        kind, j = i % N_MIXERS, i // N_MIXERS
        xp, xs = _ffn(xp, xs, n_ffn1, w1g, w1u, w1d, i, bm=FFN_ROWS)
        if kind == 0:
            proj_p, a1_p = _norm_mm(xp, n_mix, gla_w, i, j, bm=1024, bn=512, w2=gla_wa1)
            proj_s, a1_s = _norm_mm(xs, n_mix, gla_w, i, j, bm=ms, bn=512, w2=gla_wa1)
            pad_s = lambda a: jnp.pad(a.reshape(bs, ts, -1), ((0, 0), (0, GLA_CHUNK - ts), (0, 0)))
            s0 = jnp.zeros((bp, GLA_HEADS, dk, dv), f32)
            og_p, st_p = _gla(proj_p.reshape(bp, t, -1), a1_p.reshape(bp, t, -1), gla_wa2, gla_ba, gla_gn,
                              s0, j, tb=256, valid=GLA_CHUNK)
            og_s, st_s = _gla(pad_s(proj_s), pad_s(a1_s), gla_wa2, gla_ba, gla_gn,
                              state_gla[j], j, tb=GLA_CHUNK, valid=ts)
            xp = _mm_res(og_p.reshape(mp, -1), gla_wo, xp, j, bm=1024, bn=1024)
            xs = _mm_res(og_s[:, :ts].reshape(ms, -1), gla_wo, xs, j, bm=ms, bn=1024)
            gla_p.append(st_p)
            gla_s.append(st_s)
        elif kind == 1:
            q_p, k_p, v_p = [a.reshape(bp, t, d)
                             for a in _norm_mm_split(xp, n_mix, sb_wi, i, j, bm=1024, bn=512, nsplit=3)]
            qkv_s = _norm_mm(xs, n_mix, sb_wi, i, j, bm=ms, bn=512)
            bias_p = jnp.broadcast_to(sb_bias[j][:, None, None], (SB_HEADS, 1, SB_KEYS))
            o_p = _sb_prompt(q_p, k_p, v_p, bias_p, SB_HEADS)
            k_s = qkv_s[:, d:2 * d].reshape(bs, ts, d)
            v_s = qkv_s[:, 2 * d:].reshape(bs, ts, d)
            pad = ((0, 0), (0, page - ts), (0, 0))
            bias_s = jnp.repeat(sb_bias[j], ts)[None, :]
            o_s = _sb_sample(qkv_s[:, :d].reshape(bs, ts, d), jnp.pad(k_s, pad), jnp.pad(v_s, pad),
                             cache_sb_k.reshape(cache_sb_k.shape[:2] + (page * SB_HEADS, hd)),
                             cache_sb_v.reshape(cache_sb_v.shape[:2] + (page * SB_HEADS, hd)),
                             page_table, bias_s, j, SB_HEADS)
            xp = _mm_res(o_p.reshape(mp, d), sb_wo, xp, j, bm=1024, bn=1024)
            xs = _mm_res(o_s.reshape(ms, d), sb_wo, xs, j, bm=ms, bn=1024)
            kp.append(k_p.reshape(bp, t, SB_HEADS, hd))
            vp.append(v_p.reshape(bp, t, SB_HEADS, hd))
            ksm.append(k_s.reshape(bs, ts, SB_HEADS, hd))
            vsm.append(v_s.reshape(bs, ts, SB_HEADS, hd))
        else:
            st0_p = jnp.zeros((bp, POOL_HALO, d), f32)
            st0_s = jnp.pad(state_pool[j], ((0, 0), (POOL_HALO - pstate, 0), (0, 0)))
            xp3, so_p = _pool(xp.reshape(bp, t, d), st0_p, n_mix, pool_wb, pool_sc, i, j, bm=512, pos0=0)
            xs3, so_s = _pool(xs.reshape(bs, ts, d), st0_s, n_mix, pool_wb, pool_sc, i, j, bm=ts,
                              pos0=page_table.shape[1] * page)
            xp, xs = xp3.reshape(mp, d), xs3.reshape(ms, d)
            pool_p.append(so_p[:, POOL_HALO - pstate:])
            pool_s.append(so_s[:, POOL_HALO - pstate:])
        xp, xs = _ffn(xp, xs, n_ffn2, w2g, w2u, w2d, i, bm=FFN_ROWS)
        final = i == depth - 1
        xp = _ple(xp, pp, n_ple, wpg, wpp, n_final, i, bm=512, final=final)
        xs = _ple(xs, psm, n_ple, wpg, wpp, n_final, i, bm=ms, final=final)
    return (xp.reshape(bp, t, d), xs.reshape(bs, ts, d), jnp.stack(gla_p), jnp.stack(gla_s),
            jnp.stack(kp), jnp.stack(vp), jnp.stack(ksm), jnp.stack(vsm),
            jnp.stack(pool_p), jnp.stack(pool_s))
```

```python
import functools

import jax
import jax.numpy as jnp
from jax import lax
from jax.experimental import pallas as pl
from jax.experimental.pallas import tpu as pltpu

f32 = jnp.float32
bf16 = jnp.bfloat16

EPS = 1e-6
N_MIXERS = 3
GLA_HEADS = 4
GLA_TAU = 16.0
GLA_CHUNK = 64
GLA_RANK_PAD = 128
SB_HEADS = 16
SB_KEYS = 256
SB_SUB = 128
POOL_WINDOWS = (2, 4, 8, 16)
POOL_HALO = 16
VMEM_LIMIT = 58 * 1024 * 1024
FFN_ROWS = 512


def _cparams(sem):
    return pltpu.CompilerParams(dimension_semantics=sem, vmem_limit_bytes=VMEM_LIMIT)


def _rms(x, g):
    return x * lax.rsqrt(jnp.mean(x * x, axis=-1, keepdims=True) + EPS) * g


def _ffn_kernel(x_ref, xs_ref, g_ref, wg_ref, wu_ref, wd_ref, o_ref, os_ref, h_ref):
    i = pl.program_id(0)
    j = pl.program_id(1)
    bm = x_ref.shape[0]
    ms = xs_ref.shape[0]

    def swiglu_down(h):
        a = jnp.dot(h, wg_ref[...], preferred_element_type=f32)
        u = jnp.dot(h, wu_ref[...], preferred_element_type=f32)
        act = (0.5 * a * jax.nn.sigmoid(a) * u).astype(bf16)
        return jnp.dot(act, wd_ref[...], preferred_element_type=f32)

    @pl.when(j == 0)
    def _():
        x = x_ref[...]
        h_ref[0:bm, :] = _rms(x, g_ref[...]).astype(bf16)
        o_ref[...] = x

    @pl.when((j == 0) & (i == 0))
    def _():
        xs = xs_ref[...]
        h_ref[bm:bm + ms, :] = _rms(xs, g_ref[...]).astype(bf16)
        os_ref[...] = xs

    @pl.when(i == 0)
    def _():
        y = swiglu_down(h_ref[...])
        o_ref[...] += y[:bm]
        os_ref[...] += y[bm:]

    @pl.when(i != 0)
    def _():
        o_ref[...] += swiglu_down(h_ref[0:bm, :])


def _ffn(x, xs, g, wg, wu, wd, layer, bm, bf=512):
    m, d = x.shape
    ms = xs.shape[0]
    f = wg.shape[-1]
    return pl.pallas_call(
        _ffn_kernel,
        grid=(m // bm, f // bf),
        in_specs=[
            pl.BlockSpec((bm, d), lambda i, j: (i, 0)),
            pl.BlockSpec((ms, d), lambda i, j: (0, 0)),
            pl.BlockSpec((None, 1, d), lambda i, j: (layer, 0, 0)),
            pl.BlockSpec((None, d, bf), lambda i, j: (layer, 0, j)),
            pl.BlockSpec((None, d, bf), lambda i, j: (layer, 0, j)),
            pl.BlockSpec((None, bf, d), lambda i, j: (layer, j, 0)),
        ],
        out_specs=[
            pl.BlockSpec((bm, d), lambda i, j: (i, 0)),
            pl.BlockSpec((ms, d), lambda i, j: (0, 0)),
        ],
        out_shape=[jax.ShapeDtypeStruct((m, d), f32), jax.ShapeDtypeStruct((ms, d), f32)],
        scratch_shapes=[pltpu.VMEM((bm + ms, d), bf16)],
        compiler_params=_cparams(("arbitrary", "arbitrary")),
        name="ffn",
    )(x, xs, g, wg, wu, wd)


def _ple_kernel(x_ref, p_ref, g_ref, wg_ref, wp_ref, gf_ref, o_ref, *, final):
    x = x_ref[...]
    h = _rms(x, g_ref[...]).astype(bf16)
    gate = jax.nn.sigmoid(jnp.dot(h, wg_ref[...], preferred_element_type=f32))
    proj = jnp.dot(p_ref[...].astype(bf16), wp_ref[...], preferred_element_type=f32)
    y = x + gate * proj
    if final:
        y = _rms(y, gf_ref[...])
    o_ref[...] = y


def _ple(x, p, g, wg, wp, gf, layer, bm, final):
    m, d = x.shape
    pd = p.shape[-1]
    return pl.pallas_call(
        functools.partial(_ple_kernel, final=final),
        grid=(m // bm,),
        in_specs=[
            pl.BlockSpec((bm, d), lambda i: (i, 0)),
            pl.BlockSpec((None, bm, pd), lambda i: (layer, i, 0)),
            pl.BlockSpec((None, 1, d), lambda i: (layer, 0, 0)),
            pl.BlockSpec((None, d, d), lambda i: (layer, 0, 0), pipeline_mode=pl.Buffered(1)),
            pl.BlockSpec((None, pd, d), lambda i: (layer, 0, 0), pipeline_mode=pl.Buffered(1)),
            pl.BlockSpec((1, d), lambda i: (0, 0)),
        ],
        out_specs=pl.BlockSpec((bm, d), lambda i: (i, 0)),
        out_shape=jax.ShapeDtypeStruct((m, d), f32),
        compiler_params=_cparams(("parallel",)),
        name="ple",
    )(x, p, g, wg, wp, gf)


def _norm_mm_kernel(*refs, has_aux):
    if has_aux:
        x_ref, g_ref, w_ref, w2_ref, o_ref, o2_ref, h_ref = refs
    else:
        x_ref, g_ref, w_ref, o_ref, h_ref = refs

    @pl.when(pl.program_id(1) == 0)
    def _():
        h_ref[...] = _rms(x_ref[...], g_ref[...]).astype(bf16)
        if has_aux:
            o2_ref[...] = jnp.dot(h_ref[...], w2_ref[...], preferred_element_type=f32)

    o_ref[...] = jnp.dot(h_ref[...], w_ref[...], preferred_element_type=f32)


def _norm_mm_split_kernel(x_ref, g_ref, w_ref, *rest, per):
    outs, h_ref = rest[:-1], rest[-1]
    j = pl.program_id(1)

    @pl.when(j == 0)
    def _():
        h_ref[...] = _rms(x_ref[...], g_ref[...]).astype(bf16)

    y = jnp.dot(h_ref[...], w_ref[...], preferred_element_type=f32)
    for s, o_ref in enumerate(outs):
        @pl.when((j >= s * per) & (j < (s + 1) * per))
        def _(o_ref=o_ref):
            o_ref[...] = y


def _norm_mm_split(x, g, w, layer, wlayer, bm, bn, nsplit):
    m, d = x.shape
    n = w.shape[-1]
    per = n // nsplit // bn
    assert n == nsplit * per * bn and m % bm == 0
    return pl.pallas_call(
        functools.partial(_norm_mm_split_kernel, per=per),
        grid=(m // bm, n // bn),
        in_specs=[
            pl.BlockSpec((bm, d), lambda i, j: (i, 0)),
            pl.BlockSpec((None, 1, d), lambda i, j: (layer, 0, 0)),
            pl.BlockSpec((None, d, bn), lambda i, j: (wlayer, 0, j)),
        ],
        out_specs=[pl.BlockSpec((bm, bn), lambda i, j, s=s: (i, jnp.clip(j - s * per, 0, per - 1)))
                   for s in range(nsplit)],
        out_shape=[jax.ShapeDtypeStruct((m, n // nsplit), f32)] * nsplit,
        scratch_shapes=[pltpu.VMEM((bm, d), bf16)],
        compiler_params=_cparams(("arbitrary", "arbitrary")),
        name="norm_mm_split",
    )(x, g, w)


def _norm_mm(x, g, w, layer, wlayer, bm, bn, w2=None):
    m, d = x.shape
    n = w.shape[-1]
    assert n % bn == 0 and m % bm == 0
    in_specs = [
        pl.BlockSpec((bm, d), lambda i, j: (i, 0)),
        pl.BlockSpec((None, 1, d), lambda i, j: (layer, 0, 0)),
        pl.BlockSpec((None, d, bn), lambda i, j: (wlayer, 0, j)),
    ]
    out_specs = [pl.BlockSpec((bm, bn), lambda i, j: (i, j))]
    out_shape = [jax.ShapeDtypeStruct((m, n), f32)]
    args = [x, g, w]
    if w2 is not None:
        n2 = w2.shape[-1]
        in_specs.append(pl.BlockSpec((None, d, n2), lambda i, j: (wlayer, 0, 0)))
        out_specs.append(pl.BlockSpec((bm, n2), lambda i, j: (i, 0)))
        out_shape.append(jax.ShapeDtypeStruct((m, n2), f32))
        args.append(w2)
    outs = pl.pallas_call(
        functools.partial(_norm_mm_kernel, has_aux=w2 is not None),
        grid=(m // bm, n // bn),
        in_specs=in_specs,
        out_specs=out_specs,
        out_shape=out_shape,
        scratch_shapes=[pltpu.VMEM((bm, d), bf16)],
        compiler_params=_cparams(("parallel", "arbitrary")),
        name="norm_mm",
    )(*args)
    return outs if w2 is not None else outs[0]


def _mm_res_kernel(a_ref, w_ref, x_ref, o_ref):
    o_ref[...] = x_ref[...] + jnp.dot(a_ref[...], w_ref[...], preferred_element_type=f32)


def _mm_res(a, w, x, wlayer, bm, bn):
    m, k = a.shape
    n = w.shape[-1]
    return pl.pallas_call(
        _mm_res_kernel,
        grid=(m // bm, n // bn),
        in_specs=[
            pl.BlockSpec((bm, k), lambda i, j: (i, 0)),
            pl.BlockSpec((None, k, bn), lambda i, j: (wlayer, 0, j)),
            pl.BlockSpec((bm, bn), lambda i, j: (i, j)),
        ],
        out_specs=pl.BlockSpec((bm, bn), lambda i, j: (i, j)),
        out_shape=jax.ShapeDtypeStruct((m, n), f32),
        compiler_params=_cparams(("parallel", "arbitrary")),
        name="mm_res",
    )(a, w, x)


def _split3(x):
    hi = x.astype(bf16)
    r1 = x - hi.astype(f32)
    mid = r1.astype(bf16)
    lo = (r1 - mid.astype(f32)).astype(bf16)
    return hi, mid, lo


def _gla_kernel(q_ref, k_ref, v_ref, r_ref, a1_ref, wa2_ref, ba_ref, gn_ref, s0_ref,
                og_ref, sout_ref, s_ref, *, chunk, n_chunks, valid, qscale):
    c_rows = chunk
    heads, dk, dv = s_ref.shape

    @pl.when(pl.program_id(1) == 0)
    def _():
        s_ref[...] = s0_ref[0]

    row = lax.broadcasted_iota(jnp.int32, (c_rows, c_rows), 0)
    col = lax.broadcasted_iota(jnp.int32, (c_rows, c_rows), 1)
    tril = col <= row
    tri_b = tril.astype(bf16)
    ones_b = jnp.ones((c_rows, 128), bf16)
    wa2 = wa2_ref[...]
    ba = ba_ref[...]
    gn = gn_ref[...]
    tn = (((0,), (0,)), ((), ()))
    nt = (((1,), (1,)), ((), ()))

    def body(c, carry):
        sl = pl.ds(pl.multiple_of(c * c_rows, c_rows), c_rows)
        x = jnp.dot(a1_ref[0, sl, :].astype(bf16), wa2, preferred_element_type=f32) + ba
        la = (jnp.minimum(x, 0.0) - jnp.log1p(jnp.exp(-jnp.abs(x)))) * (1.0 / GLA_TAU)
        if valid < c_rows:
            la = jnp.where(lax.broadcasted_iota(jnp.int32, la.shape, 0) < valid, la, 0.0)
        parts = _split3(la)
        b = sum(jnp.dot(tri_b, p, preferred_element_type=f32) for p in parts)
        b_last = b[c_rows - 1:c_rows, :]
        q = q_ref[0, sl, :] * qscale
        k = k_ref[0, sl, :]
        vb = v_ref[0, sl, :].astype(bf16)
        r = r_ref[0, sl, :]
        qd = (q * jnp.exp(b)).astype(bf16)
        ki = (k * jnp.exp(-b)).astype(bf16)
        ke = (k * jnp.exp(b_last - b)).astype(bf16)
        for h in range(heads):
            kc = slice(h * dk, (h + 1) * dk)
            vc = slice(h * dv, (h + 1) * dv)
            att = lax.dot_general(qd[:, kc], ki[:, kc], nt, preferred_element_type=f32)
            att = jnp.where(tril, att, 0.0).astype(bf16)
            s_old = s_ref[h]
            o = (jnp.dot(att, vb[:, vc], preferred_element_type=f32)
                 + jnp.dot(qd[:, kc], s_old.astype(bf16), preferred_element_type=f32))
            dl_col = sum(lax.dot_general(p[:, kc], ones_b, tn, preferred_element_type=f32) for p in parts)
            decay = jnp.concatenate([jnp.exp(dl_col)] * (dv // 128), axis=1)
            s_ref[h] = s_old * decay + lax.dot_general(ke[:, kc], vb[:, vc], tn, preferred_element_type=f32)
            o = o * lax.rsqrt(jnp.mean(o * o, axis=-1, keepdims=True) + EPS) * gn[:, vc]
            rh = r[:, vc]
            og_ref[0, sl, vc] = (o * (rh * jax.nn.sigmoid(rh))).astype(og_ref.dtype)
        return carry

    lax.fori_loop(0, n_chunks, body, 0)

    @pl.when(pl.program_id(1) == pl.num_programs(1) - 1)
    def _():
        sout_ref[0] = s_ref[...]


def _gla(proj, a1, wa2, ba, gn, s0, layer, tb, valid):
    bsz, t, _ = proj.shape
    heads = GLA_HEADS
    dk, dv = s0.shape[-2], s0.shape[-1]
    kd, vd = heads * dk, heads * dv
    assert vd == 2 * kd
    kern = functools.partial(_gla_kernel, chunk=GLA_CHUNK, n_chunks=tb // GLA_CHUNK, valid=valid,
                             qscale=float(dk) ** -0.5)
    return pl.pallas_call(
        kern,
        grid=(bsz, t // tb),
        in_specs=[
            pl.BlockSpec((1, tb, kd), lambda b, i: (b, i, 0)),
            pl.BlockSpec((1, tb, kd), lambda b, i: (b, i, 1)),
            pl.BlockSpec((1, tb, vd), lambda b, i: (b, i, 1)),
            pl.BlockSpec((1, tb, vd), lambda b, i: (b, i, 2)),
            pl.BlockSpec((1, tb, GLA_RANK_PAD), lambda b, i: (b, i, 0)),
            pl.BlockSpec((None, GLA_RANK_PAD, kd), lambda b, i: (layer, 0, 0)),
            pl.BlockSpec((None, 1, kd), lambda b, i: (layer, 0, 0)),
            pl.BlockSpec((None, 1, vd), lambda b, i: (layer, 0, 0)),
            pl.BlockSpec((1, heads, dk, dv), lambda b, i: (b, 0, 0, 0)),
        ],
        out_specs=[
            pl.BlockSpec((1, tb, vd), lambda b, i: (b, i, 0)),
            pl.BlockSpec((1, heads, dk, dv), lambda b, i: (b, 0, 0, 0)),
        ],
        out_shape=[
            jax.ShapeDtypeStruct((bsz, t, vd), bf16),
            jax.ShapeDtypeStruct((bsz, heads, dk, dv), f32),
        ],
        scratch_shapes=[pltpu.VMEM((heads, dk, dv), f32)],
        compiler_params=_cparams(("parallel", "arbitrary")),
        name="gla",
    )(proj, proj, proj, proj, a1, wa2, ba, gn, s0)


def _softplus_both(z):
    sp = jnp.maximum(z, 0.0) + jnp.log(1.0 + jnp.exp(-jnp.abs(z)))
    return sp, sp - z


def _sbp_kernel(q_ref, k_ref, v_ref, bias_ref, o_ref, *, bq, hd, scale):
    i = pl.program_id(2)
    nh = q_ref.shape[-1] // hd
    qs = [(q_ref[0, :, h * hd:(h + 1) * hd] * scale).astype(bf16) for h in range(nh)]
    biases = [bias_ref[h] for h in range(nh)]
    sub = SB_SUB
    rj = lax.broadcasted_iota(jnp.int32, (sub, 2 * sub), 0)
    cs = lax.broadcasted_iota(jnp.int32, (sub, 2 * sub), 1)
    ntri = jnp.where((cs >= sub) | (rj > cs), -1.0, 0.0).astype(bf16)
    col_minus_row = (lax.broadcasted_iota(jnp.int32, (bq, SB_KEYS), 1)
                     - lax.broadcasted_iota(jnp.int32, (bq, SB_KEYS), 0))
    nt = (((1,), (1,)), ((), ()))
    steps_per_q = bq // SB_KEYS

    def step(h, kb, carry, acc, mask_offset):
        ks = pl.ds(pl.multiple_of(kb * SB_KEYS, SB_KEYS), SB_KEYS)
        k = k_ref[0, ks, h * hd:(h + 1) * hd].astype(bf16)
        v = v_ref[0, ks, h * hd:(h + 1) * hd].astype(bf16)
        z = lax.dot_general(qs[h], k, nt, preferred_element_type=f32) + biases[h]
        sp, spn = _softplus_both(z)
        if mask_offset is not None:
            past = col_minus_row < mask_offset
            sp = jnp.where(past, sp, 0.0)
        spb = sp.astype(bf16)
        r_hi = jnp.dot(spb[:, sub:], ntri, preferred_element_type=f32)
        r_lo = jnp.dot(spb[:, :sub], ntri, preferred_element_type=f32)
        c_mid = carry + r_hi[:, sub:]
        after = jnp.concatenate([c_mid + r_lo[:, :sub], carry + r_hi[:, :sub]], axis=1)
        a = jnp.exp(after - spn)
        if mask_offset is not None:
            a = jnp.where(past, a, 0.0)
        acc = acc + jnp.dot(a.astype(bf16), v, preferred_element_type=f32)
        return c_mid + r_lo[:, sub:], acc

    state = []
    for h in range(nh):
        carry = jnp.zeros((bq, sub), f32)
        acc = jnp.zeros((bq, hd), f32)
        for d in range(steps_per_q - 1, -1, -1):
            carry, acc = step(h, i * steps_per_q + d, carry, acc, -d * SB_KEYS)
        state += [carry, acc]

    def body(n, st):
        kb = i * steps_per_q - 1 - 2 * n
        out = []
        for h in range(nh):
            c, a = step(h, kb, st[2 * h], st[2 * h + 1], None)
            out += list(step(h, kb - 1, c, a, None))
        return tuple(out)

    state = lax.fori_loop(0, i * (steps_per_q // 2), body, tuple(state))
    for h in range(nh):
        o_ref[0, :, h * hd:(h + 1) * hd] = state[2 * h + 1].astype(o_ref.dtype)


def _sb_prompt(q, k, v, bias, heads, bq=512, nh=4):
    bsz, t, d = q.shape
    hd = d // heads
    assert bq % (2 * SB_KEYS) == 0 and t % bq == 0 and heads % nh == 0
    kern = functools.partial(_sbp_kernel, bq=bq, hd=hd, scale=float(hd) ** -0.5)
    return pl.pallas_call(
        kern,
        grid=(bsz, heads // nh, t // bq),
        in_specs=[
            pl.BlockSpec((1, bq, nh * hd), lambda b, h, i: (b, i, h)),
            pl.BlockSpec((1, t, nh * hd), lambda b, h, i: (b, 0, h)),
            pl.BlockSpec((1, t, nh * hd), lambda b, h, i: (b, 0, h)),
            pl.BlockSpec((nh, 1, SB_KEYS), lambda b, h, i: (h, 0, 0)),
        ],
        out_specs=pl.BlockSpec((1, bq, nh * hd), lambda b, h, i: (b, i, h)),
        out_shape=jax.ShapeDtypeStruct((bsz, t, d), bf16),
        compiler_params=_cparams(("parallel", "parallel", "arbitrary")),
        name="sb_prompt",
    )(q, k, v, bias)


def _sbs_kernel(pt_ref, q_ref, kn_ref, vn_ref, *rest, heads, nq, scale, npp):
    k_refs, v_refs = rest[:npp], rest[npp:2 * npp]
    bias_ref, o_ref, qm_ref, acc_ref, carry_ref = rest[2 * npp:]
    p = pl.program_id(1)
    hd = k_refs[0].shape[2]
    ps = k_refs[0].shape[1] // heads
    d = heads * hd
    nc = heads * nq
    bias = bias_ref[...]
    rs = lax.broadcasted_iota(jnp.int32, (2 * ps, 2 * ps), 0)
    cj = lax.broadcasted_iota(jnp.int32, (2 * ps, 2 * ps), 1)
    tri = (cj > rs).astype(bf16)
    nt = (((1,), (1,)), ((), ()))
    tn = (((0,), (0,)), ((), ()))

    def gather(ref):
        return jnp.concatenate(
            [ref[0, pl.ds(h, ps, stride=heads), :].astype(bf16) for h in range(heads)], axis=1)

    def weights(kp, carry, tri_m, masked):
        n = kp.shape[0]
        z = lax.dot_general(kp, qm_ref[...], nt, preferred_element_type=f32) * scale + bias
        sp, spn = _softplus_both(z)
        if masked:
            key = lax.broadcasted_iota(jnp.int32, (n, nc), 0)
            qry = lax.broadcasted_iota(jnp.int32, (n, nc), 1) % nq
            past = key < qry
            sp = jnp.where(past, sp, 0.0)
        after = carry - jnp.dot(tri_m, sp.astype(bf16), preferred_element_type=f32)
        w = jnp.exp(after - spn)
        if masked:
            w = jnp.where(past, w, 0.0)
        return w.astype(bf16), carry - jnp.sum(sp, axis=0, keepdims=True)

    @pl.when(p == 0)
    def _():
        q = q_ref[0]
        qrep = jnp.concatenate([q] * heads, axis=0)
        rh = lax.broadcasted_iota(jnp.int32, (nc, d), 0) // nq
        ch = lax.broadcasted_iota(jnp.int32, (nc, d), 1) // hd
        qm_ref[...] = jnp.where(rh == ch, qrep, 0.0).astype(bf16)
        w, carry = weights(kn_ref[0].astype(bf16), jnp.zeros((1, nc), f32), tri[:ps, :ps], True)
        acc_ref[...] = lax.dot_general(w, vn_ref[0].astype(bf16), tn, preferred_element_type=f32)
        carry_ref[...] = jnp.broadcast_to(carry, carry_ref.shape)

    carry = carry_ref[0:1, :]
    ws = []
    for g in range(0, npp, 2):
        kp = jnp.concatenate([gather(k_refs[g + 1]), gather(k_refs[g])], axis=0)
        w, carry = weights(kp, carry, tri, False)
        ws.append(w)
    w_all = jnp.concatenate(ws[::-1], axis=0)
    v_all = jnp.concatenate([gather(r) for r in v_refs[::-1]], axis=0)
    acc_ref[...] += lax.dot_general(w_all, v_all, tn, preferred_element_type=f32)
    carry_ref[...] = jnp.broadcast_to(carry, carry_ref.shape)

    @pl.when(p == pl.num_programs(1) - 1)
    def _():
        for h in range(heads):
            o_ref[0, :, h * hd:(h + 1) * hd] = acc_ref[h * nq:(h + 1) * nq, h * hd:(h + 1) * hd].astype(o_ref.dtype)


def _sb_sample(q, k_new, v_new, cache_k, cache_v, page_table, bias, layer, heads, npp=8):
    bsz, nq, d = q.shape
    n_pages = page_table.shape[1]
    assert n_pages % npp == 0 and npp % 2 == 0
    hd = d // heads
    ps = cache_k.shape[2] // heads
    nc = heads * nq
    kern = functools.partial(_sbs_kernel, heads=heads, nq=nq, scale=float(hd) ** -0.5, npp=npp)
    cache_spec = lambda back: pl.BlockSpec(
        (None, 1, ps * heads, hd), lambda b, p, pt: (layer, pt[b, n_pages - back - npp * p], 0, 0))
    cache_specs = [cache_spec(r + 1) for r in range(npp)]
    grid_spec = pltpu.PrefetchScalarGridSpec(
        num_scalar_prefetch=1,
        grid=(bsz, n_pages // npp),
        in_specs=[
            pl.BlockSpec((1, nq, d), lambda b, p, pt: (b, 0, 0)),
            pl.BlockSpec((1, ps, d), lambda b, p, pt: (b, 0, 0)),
            pl.BlockSpec((1, ps, d), lambda b, p, pt: (b, 0, 0)),
            *cache_specs, *cache_specs,
            pl.BlockSpec((1, nc), lambda b, p, pt: (0, 0)),
        ],
        out_specs=pl.BlockSpec((1, nq, d), lambda b, p, pt: (b, 0, 0)),
        scratch_shapes=[
            pltpu.VMEM((nc, d), bf16),
            pltpu.VMEM((nc, d), f32),
            pltpu.VMEM((8, nc), f32),
        ],
    )
    return pl.pallas_call(
        kern,
        grid_spec=grid_spec,
        out_shape=jax.ShapeDtypeStruct((bsz, nq, d), bf16),
        compiler_params=_cparams(("parallel", "arbitrary")),
        name="sb_sample",
    )(page_table, q, k_new, v_new, *([cache_k] * npp), *([cache_v] * npp), bias)


def _pool_kernel(*refs, bm, has_halo, pos0, windows):
    if has_halo:
        x_ref, xh_ref, st_ref, g_ref, w_ref, sc_ref, o_ref, so_ref, ext_ref = refs
    else:
        x_ref, st_ref, g_ref, w_ref, sc_ref, o_ref, so_ref, ext_ref = refs
    i = pl.program_id(1)
    g = g_ref[...]
    x = x_ref[0]
    h = _rms(x, g)
    if has_halo:
        halo = jnp.where(i == 0, st_ref[0], _rms(xh_ref[0], g))
    else:
        halo = st_ref[0]
    ext_ref[0:POOL_HALO, :] = halo
    ext_ref[POOL_HALO:POOL_HALO + bm, :] = h
    gd = x.shape[-1] // len(windows)
    pos = pos0 + i * bm + lax.broadcasted_iota(jnp.int32, (bm, 1), 0)
    for gi, w in enumerate(windows):
        cs = slice(gi * gd, (gi + 1) * gd)
        hg = h[:, cs]
        win = hg
        for s in range(1, w):
            win = win + ext_ref[pl.ds(POOL_HALO - s, bm), cs]
        cnt = jnp.minimum(w, pos + 1).astype(f32)
        dlt = (win / cnt - hg).astype(bf16)
        y = jnp.dot(dlt, w_ref[gi], preferred_element_type=f32)
        o_ref[0, :, cs] = x[:, cs] + y * sc_ref[:, cs]
    so_ref[0] = ext_ref[pl.ds(bm, POOL_HALO), :]


def _pool(x, state, g, w, sc, layer, wlayer, bm, pos0):
    bsz, t, d = x.shape
    ng, gd = w.shape[1], w.shape[2]
    has_halo = t > bm
    hb = bm // POOL_HALO
    kern = functools.partial(_pool_kernel, bm=bm, has_halo=has_halo, pos0=pos0, windows=POOL_WINDOWS)
    in_specs = [pl.BlockSpec((1, bm, d), lambda b, i: (b, i, 0))]
    args = [x]
    if has_halo:
        in_specs.append(pl.BlockSpec((1, POOL_HALO, d), lambda b, i: (b, jnp.maximum(i * hb - 1, 0), 0)))
        args.append(x)
    in_specs += [
        pl.BlockSpec((1, POOL_HALO, d), lambda b, i: (b, 0, 0)),
        pl.BlockSpec((None, 1, d), lambda b, i: (layer, 0, 0)),
        pl.BlockSpec((None, ng, gd, gd), lambda b, i: (wlayer, 0, 0, 0)),
        pl.BlockSpec((None, 1, d), lambda b, i: (wlayer, 0, 0)),
    ]
    args += [state, g, w, sc]
    return pl.pallas_call(
        kern,
        grid=(bsz, t // bm),
        in_specs=in_specs,
        out_specs=[
            pl.BlockSpec((1, bm, d), lambda b, i: (b, i, 0)),
            pl.BlockSpec((1, POOL_HALO, d), lambda b, i: (b, 0, 0)),
        ],
        out_shape=[
            jax.ShapeDtypeStruct((bsz, t, d), f32),
            jax.ShapeDtypeStruct((bsz, POOL_HALO, d), f32),
        ],
        scratch_shapes=[pltpu.VMEM((POOL_HALO + bm, d), f32)],
        compiler_params=_cparams(("parallel", "arbitrary")),
        name="pool",
    )(*args)


def kernel(x_prompt, x_sample, state_gla, cache_sb_k, cache_sb_v, state_pool, page_table, p_prompt, p_sample, norm_ffn1, ffn1_gate, ffn1_up, ffn1_down, norm_mix, norm_ffn2, ffn2_gate, ffn2_up, ffn2_down, norm_ple, ple_gate, ple_proj, gla_in, gla_a1, gla_a2, gla_a_bias, gla_norm, gla_out, sb_in, sb_bias, sb_out, pool_w, pool_scale, norm_final):
    bp, t, d = x_prompt.shape
    bs, ts, _ = x_sample.shape
    depth = norm_ffn1.shape[0]
    mp, ms = bp * t, bs * ts
    kd = gla_a2.shape[-1]
    dk = kd // GLA_HEADS
    dv = gla_out.shape[1] // GLA_HEADS
    hd = d // SB_HEADS
    page = cache_sb_k.shape[2]
    pstate = state_pool.shape[2]

    row3 = lambda a: a.reshape(a.shape[0], 1, a.shape[-1])
    n_ffn1, n_mix, n_ffn2, n_ple = row3(norm_ffn1), row3(norm_mix), row3(norm_ffn2), row3(norm_ple)
    n_final = norm_final.reshape(1, d)
    w1g, w1u, w1d = ffn1_gate.astype(bf16), ffn1_up.astype(bf16), ffn1_down.astype(bf16)
    w2g, w2u, w2d = ffn2_gate.astype(bf16), ffn2_up.astype(bf16), ffn2_down.astype(bf16)
    wpg, wpp = ple_gate.astype(bf16), ple_proj.astype(bf16)
    n_gla = gla_in.shape[0]
    gla_w = gla_in.astype(bf16)
    gla_wa1 = jnp.pad(gla_a1, ((0, 0), (0, 0), (0, GLA_RANK_PAD - gla_a1.shape[-1]))).astype(bf16)
    gla_wa2 = jnp.concatenate(
        [gla_a2, jnp.zeros((n_gla, GLA_RANK_PAD - gla_a2.shape[1], kd), f32)], axis=1).astype(bf16)
    gla_ba = row3(gla_a_bias)
    gla_gn = gla_norm.reshape(n_gla, 1, GLA_HEADS * dv)
    gla_wo = gla_out.astype(bf16)
    sb_wi, sb_wo = sb_in.astype(bf16), sb_out.astype(bf16)
    pool_wb = pool_w.astype(bf16)
    pool_sc = row3(pool_scale)
    pp = p_prompt.reshape(depth, mp, -1)
    psm = p_sample.reshape(depth, ms, -1)

    xp = x_prompt.reshape(mp, d)
    xs = x_sample.reshape(ms, d)
    gla_p, gla_s, kp, vp, ksm, vsm, pool_p, pool_s = [], [], [], [], [], [], [], []
    for i in range(depth):
        kind, j = i % N_MIXERS, i // N_MIXERS
        xp, xs = _ffn(xp, xs, n_ffn1, w1g, w1u, w1d, i, bm=FFN_ROWS)
        if kind == 0:
            proj_p, a1_p = _norm_mm(xp, n_mix, gla_w, i, j, bm=1024, bn=512, w2=gla_wa1)
            proj_s, a1_s = _norm_mm(xs, n_mix, gla_w, i, j, bm=ms, bn=512, w2=gla_wa1)
            pad_s = lambda a: jnp.pad(a.reshape(bs, ts, -1), ((0, 0), (0, GLA_CHUNK - ts), (0, 0)))
            s0 = jnp.zeros((bp, GLA_HEADS, dk, dv), f32)
            og_p, st_p = _gla(proj_p.reshape(bp, t, -1), a1_p.reshape(bp, t, -1), gla_wa2, gla_ba, gla_gn,
                              s0, j, tb=256, valid=GLA_CHUNK)
            og_s, st_s = _gla(pad_s(proj_s), pad_s(a1_s), gla_wa2, gla_ba, gla_gn,
                              state_gla[j], j, tb=GLA_CHUNK, valid=ts)
            xp = _mm_res(og_p.reshape(mp, -1), gla_wo, xp, j, bm=1024, bn=1024)
            xs = _mm_res(og_s[:, :ts].reshape(ms, -1), gla_wo, xs, j, bm=ms, bn=1024)
            gla_p.append(st_p)
            gla_s.append(st_s)
        elif kind == 1:
            q_p, k_p, v_p = [a.reshape(bp, t, d)
                             for a in _norm_mm_split(xp, n_mix, sb_wi, i, j, bm=1024, bn=512, nsplit=3)]
            qkv_s = _norm_mm(xs, n_mix, sb_wi, i, j, bm=ms, bn=512)
            bias_p = jnp.broadcast_to(sb_bias[j][:, None, None], (SB_HEADS, 1, SB_KEYS))
            o_p = _sb_prompt(q_p, k_p, v_p, bias_p, SB_HEADS)
            k_s = qkv_s[:, d:2 * d].reshape(bs, ts, d)
            v_s = qkv_s[:, 2 * d:].reshape(bs, ts, d)
            pad = ((0, 0), (0, page - ts), (0, 0))
            bias_s = jnp.repeat(sb_bias[j], ts)[None, :]
            o_s = _sb_sample(qkv_s[:, :d].reshape(bs, ts, d), jnp.pad(k_s, pad), jnp.pad(v_s, pad),
                             cache_sb_k.reshape(cache_sb_k.shape[:2] + (page * SB_HEADS, hd)),
                             cache_sb_v.reshape(cache_sb_v.shape[:2] + (page * SB_HEADS, hd)),
                             page_table, bias_s, j, SB_HEADS)
            xp = _mm_res(o_p.reshape(mp, d), sb_wo, xp, j, bm=1024, bn=1024)
            xs = _mm_res(o_s.reshape(ms, d), sb_wo, xs, j, bm=ms, bn=1024)
            kp.append(k_p.reshape(bp, t, SB_HEADS, hd))
            vp.append(v_p.reshape(bp, t, SB_HEADS, hd))
            ksm.append(k_s.reshape(bs, ts, SB_HEADS, hd))
            vsm.append(v_s.reshape(bs, ts, SB_HEADS, hd))
        else:
            st0_p = jnp.zeros((bp, POOL_HALO, d), f32)
            st0_s = jnp.pad(state_pool[j], ((0, 0), (POOL_HALO - pstate, 0), (0, 0)))
            xp3, so_p = _pool(xp.reshape(bp, t, d), st0_p, n_mix, pool_wb, pool_sc, i, j, bm=512, pos0=0)
            xs3, so_s = _pool(xs.reshape(bs, ts, d), st0_s, n_mix, pool_wb, pool_sc, i, j, bm=ts,
                              pos0=page_table.shape[1] * page)
            xp, xs = xp3.reshape(mp, d), xs3.reshape(ms, d)
            pool_p.append(so_p[:, POOL_HALO - pstate:])
            pool_s.append(so_s[:, POOL_HALO - pstate:])
        xp, xs = _ffn(xp, xs, n_ffn2, w2g, w2u, w2d, i, bm=FFN_ROWS)
        final = i == depth - 1
        xp = _ple(xp, pp, n_ple, wpg, wpp, n_final, i, bm=512, final=final)
        xs = _ple(xs, psm, n_ple, wpg, wpp, n_final, i, bm=ms, final=final)
    return (xp.reshape(bp, t, d), xs.reshape(bs, ts, d), jnp.stack(gla_p), jnp.stack(gla_s),
            jnp.stack(kp), jnp.stack(vp), jnp.stack(ksm), jnp.stack(vsm),
            jnp.stack(pool_p), jnp.stack(pool_s))
```
